```python
import math
import jax, jax.numpy as jnp
from jax import lax
import numpy as np

D_MODEL = 1024
BATCH = 4
SEQ = 8192
DEPTH = 2
DEC_BATCH = 128
DEC_SEQ = 4
PAST_LEN = 16384
PAGE_SIZE = 128

HEAD_DIM = 64
N_AB = (DEPTH + 1) // 2
N_CD = DEPTH // 2
MLA_HEADS = 8
MLA_Q_LORA = 256
MLA_KV_LORA = 128
MLA_NOPE = 64
MLA_ROPE = 32
MLA_V = 64
ROPE_THETA = 10000.0
FOX_HEADS = 8
FOX_KV_HEADS = 2
NSA_HEADS = 8
NSA_KV_HEADS = 2
CMP_LEN = 32
CMP_STRIDE = 16
CMP_HIDDEN = 128
SEL_BLOCK = 64
SEL_TOPN = 16
WINDOW = 512
MOBA_HEADS = 8
MOBA_KV_HEADS = 2
MOBA_BLOCK = 256
MOBA_TOPK = 3
D_FF = 4 * D_MODEL
Q_BLK = 128
NSA_QBLK = 64
MOBA_QBLK = 32
NORM_EPS = 1e-6
MLA_W = [MLA_Q_LORA, MLA_KV_LORA, MLA_ROPE]
FOX_W = [FOX_HEADS * HEAD_DIM, FOX_KV_HEADS * HEAD_DIM, FOX_KV_HEADS * HEAD_DIM, FOX_HEADS]
AB_IN = sum(MLA_W) + sum(FOX_W)
AB_OUT = MLA_HEADS * MLA_V + FOX_HEADS * HEAD_DIM
NSA_W = [NSA_HEADS * HEAD_DIM] + [NSA_KV_HEADS * HEAD_DIM] * 6 + [3 * NSA_HEADS]
MOBA_W = [MOBA_HEADS * HEAD_DIM, MOBA_KV_HEADS * HEAD_DIM, MOBA_KV_HEADS * HEAD_DIM]
CD_IN = sum(NSA_W) + sum(MOBA_W)
CD_OUT = NSA_HEADS * HEAD_DIM + MOBA_HEADS * HEAD_DIM
F32 = jnp.float32

kernel_name = 'hybrid_mla_fox_nsa_moba_decode_step'


def rms_norm(x, g):
    xf = x.astype(F32)
    y = xf * lax.rsqrt(jnp.mean(xf * xf, axis=-1, keepdims=True) + NORM_EPS)
    return (y * g.astype(F32)).astype(x.dtype)


def rope(x, pos):
    half = x.shape[-1] // 2
    inv = jnp.power(ROPE_THETA, -jnp.arange(half, dtype=F32) / half)
    ang = pos.astype(F32)[:, None] * inv[None, :]
    cos = jnp.cos(ang)[None, :, None, :]
    sin = jnp.sin(ang)[None, :, None, :]
    xf = x.astype(F32)
    x1, x2 = xf[..., :half], xf[..., half:]
    return jnp.concatenate([x1 * cos - x2 * sin, x2 * cos + x1 * sin], axis=-1).astype(x.dtype)


def masked_softmax(logits, mask):
    logits = jnp.where(mask, logits.astype(F32), -jnp.inf)
    m = jnp.max(logits, axis=-1, keepdims=True)
    m = jnp.where(jnp.isfinite(m), m, 0.0)
    p = jnp.where(mask, jnp.exp(logits - m), 0.0)
    return p / jnp.maximum(jnp.sum(p, axis=-1, keepdims=True), 1e-30)


def split_cols(z, widths):
    offs = [int(o) for o in np.cumsum(widths)[:-1]]
    return jnp.split(z, offs, axis=-1)


def dsl(x, start, size):
    return lax.dynamic_slice_in_dim(x, start, size, axis=1)


def sweep_queries(block_fn, n_q, blk):
    outs = lax.map(block_fn, jnp.arange(n_q // blk) * blk)
    return jnp.swapaxes(outs, 0, 1).reshape(outs.shape[1], n_q, outs.shape[-1])


def dense_attend(q, k, v, qpos, kpos, bias=None):
    b, tq, h, d = q.shape
    hk = k.shape[2]
    qg = q.reshape(b, tq, hk, h // hk, d)
    logits = jnp.einsum('bqkgd,bskd->bkgqs', qg, k, preferred_element_type=F32) / math.sqrt(d)
    if bias is not None:
        logits = logits + bias
    p = masked_softmax(logits, kpos[None, :] <= qpos[:, None])
    out = jnp.einsum('bkgqs,bskd->bqkgd', p.astype(v.dtype), v)
    return out.reshape(b, tq, h * v.shape[-1])


def ab_project(h, pos, p):
    b, t, _ = h.shape
    c_q, c_kv, k_pe, fq, fk, fv, ff = split_cols(h @ p['w_in'], MLA_W + FOX_W)
    q = (rms_norm(c_q, p['g_q_lora']) @ p['w_q_up']).reshape(b, t, MLA_HEADS, MLA_NOPE + MLA_ROPE)
    mla_q = jnp.concatenate([rms_norm(q[..., :MLA_NOPE], p['g_q_nope']),
                             rope(rms_norm(q[..., MLA_NOPE:], p['g_q_rope']), pos)], axis=-1)
    k_rot = rope(rms_norm(k_pe, p['g_k_rope'])[:, :, None, :], pos)[:, :, 0, :]
    latent = jnp.concatenate([rms_norm(c_kv, p['g_kv_lora']), k_rot], axis=-1)
    fox_q = rms_norm(fq.reshape(b, t, FOX_HEADS, HEAD_DIM), p['g_fox_q'])
    fox_k = rms_norm(fk.reshape(b, t, FOX_KV_HEADS, HEAD_DIM), p['g_fox_k'])
    fox_kv = jnp.stack([fox_k, fv.reshape(b, t, FOX_KV_HEADS, HEAD_DIM)], axis=2)
    logf = jax.nn.log_sigmoid(ff.astype(F32) + p['b_fox_f'].astype(F32))
    return mla_q, latent, fox_q, fox_kv, logf


def mla_keys(latent, p):
    b, t, _ = latent.shape
    ckv, kpe = latent[..., :MLA_KV_LORA], latent[..., MLA_KV_LORA:]
    kv = (ckv @ p['w_kv_up']).reshape(b, t, MLA_HEADS, MLA_NOPE + MLA_V)
    k_nope = rms_norm(kv[..., :MLA_NOPE], p['g_k_nope'])
    k = jnp.concatenate([k_nope, jnp.broadcast_to(kpe[:, :, None, :], (b, t, MLA_HEADS, MLA_ROPE))], axis=-1)
    return k, kv[..., MLA_NOPE:]


def fox_bias(cq, ck, hk):
    b, tq, h = cq.shape
    t = ck.shape[1]
    g = h // hk
    cq = cq.reshape(b, tq, hk, g).transpose(0, 2, 3, 1)[..., None]
    ck = ck.reshape(b, t, hk, g).transpose(0, 2, 3, 1)[..., None, :]
    return cq - ck


def ab_prompt(h, p):
    b, s, _ = h.shape
    pos = jnp.arange(s)
    mla_q, latent, fq, fkv, logf = ab_project(h, pos, p)
    mk, mv = mla_keys(latent, p)
    cum = jnp.cumsum(logf, axis=1)
    fk, fv = fkv[:, :, 0], fkv[:, :, 1]

    def block(start):
        qpos = start + jnp.arange(Q_BLK)
        om = dense_attend(dsl(mla_q, start, Q_BLK), mk, mv, qpos, pos)
        of = dense_attend(dsl(fq, start, Q_BLK), fk, fv, qpos, pos,
                          fox_bias(dsl(cum, start, Q_BLK), cum, FOX_KV_HEADS))
        return jnp.concatenate([om, of], axis=-1)

    o = sweep_queries(block, s, Q_BLK)
    return o @ p['w_out'], latent, fkv, logf


def ab_sample(h, p, li, cache_mla, cache_fox_kv, cache_fox_logf, page_table):
    pos = PAST_LEN + jnp.arange(DEC_SEQ)
    kpos = jnp.arange(PAST_LEN + DEC_SEQ)
    mla_q, latent, fq, fkv, logf = ab_project(h, pos, p)

    def past(c, pages):
        return c[li, pages].reshape((1, PAST_LEN) + c.shape[3:])

    def per_seq(args):
        qm, lat, qf, kv_new, lf_new, pages = args
        lat_all = jnp.concatenate([past(cache_mla, pages), lat[None]], axis=1)
        mk, mv = mla_keys(lat_all, p)
        om = dense_attend(qm[None], mk, mv, pos, kpos)
        kv_all = jnp.concatenate([past(cache_fox_kv, pages), kv_new[None]], axis=1)
        lf_all = jnp.concatenate([past(cache_fox_logf, pages).astype(F32), lf_new[None]], axis=1)
        cum = jnp.cumsum(lf_all, axis=1)
        of = dense_attend(qf[None], kv_all[:, :, 0], kv_all[:, :, 1], pos, kpos,
                          fox_bias(cum[:, PAST_LEN:], cum, FOX_KV_HEADS))
        return jnp.concatenate([om, of], axis=-1)[0]

    o = lax.map(per_seq, (mla_q, latent, fq, fkv, logf, page_table))
    return o @ p['w_out'], latent, fkv, logf


def cd_project(h, p):
    b, t, _ = h.shape
    nq, kc, vc, ks, vs, kw, vw, gl, mq, mk, mv = split_cols(h @ p['w_in'], NSA_W + MOBA_W)
    kvh = lambda z: z.reshape(b, t, NSA_KV_HEADS, HEAD_DIM)
    nq = rms_norm(nq.reshape(b, t, NSA_HEADS, HEAD_DIM), p['g_nsa_q'])
    cmp_kv = jnp.stack([kvh(kc), kvh(vc)], axis=2)
    sel_kv = jnp.stack([rms_norm(kvh(ks), p['g_nsa_ks']), kvh(vs)], axis=2)
    win_kv = jnp.stack([rms_norm(kvh(kw), p['g_nsa_kw']), kvh(vw)], axis=2)
    gates = jax.nn.sigmoid(gl + p['b_gate']).reshape(b, t, NSA_HEADS, 3)
    mq = rms_norm(mq.reshape(b, t, MOBA_HEADS, HEAD_DIM), p['g_moba_q'])
    moba_kv = jnp.stack([rms_norm(mk.reshape(b, t, MOBA_KV_HEADS, HEAD_DIM), p['g_moba_k']),
                         mv.reshape(b, t, MOBA_KV_HEADS, HEAD_DIM)], axis=2)
    return nq, gates, cmp_kv, sel_kv, win_kv, mq, moba_kv


def compress(x, pe, w1, w2):
    b, t, hk, d = x.shape
    r = CMP_LEN // CMP_STRIDE
    n_cmp = (t - CMP_LEN) // CMP_STRIDE + 1
    chunks = x[:, :(n_cmp + r - 1) * CMP_STRIDE].reshape(b, n_cmp + r - 1, CMP_STRIDE, hk, d)
    blocks = jnp.concatenate([chunks[:, j:j + n_cmp] for j in range(r)], axis=2)
    blocks = blocks + pe[None, None, :, None, :]
    flat = blocks.transpose(0, 1, 3, 2, 4).reshape(b, n_cmp, hk, CMP_LEN * d)
    return jax.nn.silu(flat @ w1) @ w2


def sel_overlap(n_cmp, n_sb):
    i0 = jnp.arange(n_cmp)[:, None] * CMP_STRIDE
    j0 = jnp.arange(n_sb)[None, :] * SEL_BLOCK
    return ((i0 < j0 + SEL_BLOCK) & (i0 + CMP_LEN > j0)).astype(F32)


def nsa_prepare(cmp_kv, sel_kv, p):
    b, t = cmp_kv.shape[:2]
    kc = rms_norm(compress(cmp_kv[:, :, 0], p['pe_k'], p['w_ck1'], p['w_ck2']), p['g_nsa_kc'])
    vc = compress(cmp_kv[:, :, 1], p['pe_v'], p['w_cv1'], p['w_cv2'])
    n_cmp = kc.shape[1]
    cmp_end = jnp.arange(n_cmp) * CMP_STRIDE + CMP_LEN - 1
    n_sb = -(-t // SEL_BLOCK)
    sel = jnp.pad(sel_kv, ((0, 0), (0, n_sb * SEL_BLOCK - t), (0, 0), (0, 0), (0, 0)))
    sel = sel.reshape(b, n_sb, SEL_BLOCK, 2, NSA_KV_HEADS, HEAD_DIM).transpose(3, 0, 4, 1, 2, 5)
    return kc, vc, cmp_end, sel[0], sel[1], sel_overlap(n_cmp, n_sb)


def nsa_attend(q, qpos, gates, kc, vc, cmp_end, ksb, vsb, overlap, kw, vw, wpos, slopes):
    b, tq, h, d = q.shape
    hk = kc.shape[2]
    g = h // hk
    scale = 1.0 / math.sqrt(d)
    qg = q.reshape(b, tq, hk, g, d)
    sl = slopes.reshape(hk, g).astype(F32)
    tpos = qpos.astype(F32)
    lc = jnp.einsum('bqkgd,bnkd->bkgqn', qg, kc, preferred_element_type=F32) * scale
    lc = lc - sl[None, :, :, None, None] * (tpos[:, None] - cmp_end.astype(F32)[None, :])
    pc = masked_softmax(lc, cmp_end[None, :] <= qpos[:, None])
    o_cmp = jnp.einsum('bkgqn,bnkd->bqkgd', pc.astype(vc.dtype), vc)
    imp = jnp.einsum('bkgqn,nj->bkqj', pc, overlap)
    n_sb = ksb.shape[2]
    sb = jnp.arange(n_sb)[None, :]
    cur = (qpos // SEL_BLOCK)[:, None]
    forced = (sb == 0) | (sb == cur) | (sb == cur - 1)
    started = sb * SEL_BLOCK <= qpos[:, None]
    score = jnp.where(started, jnp.where(forced, jnp.inf, imp), -jnp.inf)
    _, idx = lax.top_k(score, min(SEL_TOPN, n_sb))
    bi = jnp.arange(b)[:, None, None, None]
    hi = jnp.arange(hk)[None, :, None, None]
    ks = ksb[bi, hi, idx]
    vs = vsb[bi, hi, idx]
    spos = idx[..., None] * SEL_BLOCK + jnp.arange(SEL_BLOCK)
    ls = jnp.einsum('bqkgd,bkqnsd->bkgqns', qg, ks, preferred_element_type=F32) * scale
    ls = ls - sl[None, :, :, None, None, None] * (tpos[None, None, None, :, None, None] - spos[:, :, None].astype(F32))
    ms = jnp.broadcast_to((spos <= qpos[None, None, :, None, None])[:, :, None], ls.shape)
    n_sel = idx.shape[-1]
    flat_shape = (b, hk, g, tq, n_sel * SEL_BLOCK)
    ps = masked_softmax(ls.reshape(flat_shape), ms.reshape(flat_shape)).reshape(ls.shape)
    o_sel = jnp.einsum('bkgqns,bkqnsd->bqkgd', ps.astype(vs.dtype), vs)
    lw = jnp.einsum('bqkgd,bskd->bkgqs', qg, kw, preferred_element_type=F32) * scale
    lw = lw - sl[None, :, :, None, None] * (tpos[:, None] - wpos.astype(F32)[None, :])
    dist = qpos[:, None] - wpos[None, :]
    pw = masked_softmax(lw, (dist >= 0) & (dist < WINDOW) & (wpos[None, :] >= 0))
    o_win = jnp.einsum('bkgqs,bskd->bqkgd', pw.astype(vw.dtype), vw)
    gt = gates.reshape(b, tq, hk, g, 3)
    o = gt[..., 0:1] * o_cmp + gt[..., 1:2] * o_sel + gt[..., 2:3] * o_win
    return o.reshape(b, tq, h * d)


def moba_prepare(kv):
    b, t = kv.shape[:2]
    nb = -(-t // MOBA_BLOCK)
    kvp = jnp.pad(kv, ((0, 0), (0, nb * MOBA_BLOCK - t), (0, 0), (0, 0), (0, 0)))
    kvp = kvp.reshape(b, nb, MOBA_BLOCK, 2, MOBA_KV_HEADS, HEAD_DIM).transpose(3, 0, 4, 1, 2, 5)
    kb, vb = kvp[0], kvp[1]
    kmean = jnp.mean(kb.astype(F32), axis=3)
    return kb, vb, kmean


def moba_attend(q, qpos, kb, vb, kmean, slopes):
    b, tq, h, d = q.shape
    hk, nb = kb.shape[1], kb.shape[2]
    g = h // hk
    qg = q.reshape(b, tq, hk, g, d)
    sc = jnp.einsum('bqkgd,bknd->bqkgn', qg.astype(F32), kmean)
    own = qpos // MOBA_BLOCK
    past = jnp.arange(nb)[None, :] < own[:, None]
    sc = jnp.where(past[None, :, None, None, :], sc, -jnp.inf)
    _, idx = lax.top_k(sc, min(MOBA_TOPK, nb))
    ok = idx < own[None, :, None, None, None]
    own_b = jnp.broadcast_to(own[None, :, None, None, None], (b, tq, hk, g, 1))
    idx_all = jnp.concatenate([idx, own_b], axis=-1)
    ok_all = jnp.concatenate([ok, jnp.ones(own_b.shape, bool)], axis=-1)
    bi = jnp.arange(b)[:, None, None, None, None]
    hi = jnp.arange(hk)[None, None, :, None, None]
    kg = kb[bi, hi, idx_all]
    vg = vb[bi, hi, idx_all]
    spos = idx_all[..., None] * MOBA_BLOCK + jnp.arange(MOBA_BLOCK)
    qp = qpos[None, :, None, None, None, None]
    mask = ok_all[..., None] & (spos <= qp)
    logits = jnp.einsum('bqkgd,bqkgnsd->bqkgns', qg, kg, preferred_element_type=F32) / math.sqrt(d)
    logits = logits - slopes.reshape(hk, g).astype(F32)[None, None, :, :, None, None] * (qp - spos).astype(F32)
    n = idx_all.shape[-1]
    flat_shape = (b, tq, hk, g, n * MOBA_BLOCK)
    pr = masked_softmax(logits.reshape(flat_shape), mask.reshape(flat_shape)).reshape(logits.shape)
    out = jnp.einsum('bqkgns,bqkgnsd->bqkgd', pr.astype(vg.dtype), vg)
    return out.reshape(b, tq, h * d)


def cd_prompt(h, p, slopes_n, slopes_m):
    b, s, _ = h.shape
    nq, gates, cmp_kv, sel_kv, win_kv, mq, moba_kv = cd_project(h, p)
    kc, vc, cmp_end, ksb, vsb, overlap = nsa_prepare(cmp_kv, sel_kv, p)
    win_pad = jnp.pad(win_kv, ((0, 0), (WINDOW, 0), (0, 0), (0, 0), (0, 0)))

    def nsa_block(start):
        qpos = start + jnp.arange(NSA_QBLK)
        wkv = dsl(win_pad, start, WINDOW + NSA_QBLK)
        wpos = start - WINDOW + jnp.arange(WINDOW + NSA_QBLK)
        return nsa_attend(dsl(nq, start, NSA_QBLK), qpos, dsl(gates, start, NSA_QBLK), kc, vc, cmp_end,
                          ksb, vsb, overlap, wkv[:, :, 0], wkv[:, :, 1], wpos, slopes_n)

    kb, vb, kmean = moba_prepare(moba_kv)

    def moba_block(start):
        qpos = start + jnp.arange(MOBA_QBLK)
        return moba_attend(dsl(mq, start, MOBA_QBLK), qpos, kb, vb, kmean, slopes_m)

    o = jnp.concatenate([sweep_queries(nsa_block, s, NSA_QBLK), sweep_queries(moba_block, s, MOBA_QBLK)], axis=-1)
    return o @ p['w_out'], cmp_kv, sel_kv, win_kv[:, -min(WINDOW, s):], moba_kv


def cd_sample(h, p, li, cache_nsa_cmp_kv, cache_nsa_sel_kv, state_nsa_win_kv, cache_moba_kv, page_table, slopes_n, slopes_m):
    pos = PAST_LEN + jnp.arange(DEC_SEQ)
    nq, gates, cmp_kv, sel_kv, win_kv, mq, moba_kv = cd_project(h, p)
    win_buf = state_nsa_win_kv.shape[2]
    win_all = jnp.concatenate([state_nsa_win_kv[li], win_kv], axis=1)
    wpos = PAST_LEN - win_buf + jnp.arange(win_buf + DEC_SEQ)

    def past(c, pages):
        return c[li, pages].reshape((1, PAST_LEN) + c.shape[3:])

    def per_seq(args):
        q1, g1, ckv1, skv1, wkv1, mq1, mkv1, pages = args
        cmp_all = jnp.concatenate([past(cache_nsa_cmp_kv, pages), ckv1[None]], axis=1)
        sel_all = jnp.concatenate([past(cache_nsa_sel_kv, pages), skv1[None]], axis=1)
        kc, vc, cmp_end, ksb, vsb, overlap = nsa_prepare(cmp_all, sel_all, p)
        o_n = nsa_attend(q1[None], pos, g1[None], kc, vc, cmp_end, ksb, vsb, overlap,
                         wkv1[None, :, 0], wkv1[None, :, 1], wpos, slopes_n)
        kb, vb, kmean = moba_prepare(jnp.concatenate([past(cache_moba_kv, pages), mkv1[None]], axis=1))
        o_m = moba_attend(mq1[None], pos, kb, vb, kmean, slopes_m)
        return jnp.concatenate([o_n, o_m], axis=-1)[0]

    o = lax.map(per_seq, (nq, gates, cmp_kv, sel_kv, win_all, mq, moba_kv, page_table))
    return o @ p['w_out'], cmp_kv, sel_kv, win_all[:, -win_buf:], moba_kv


def sq_relu_mlp(x, g, w_up, w_down):
    u = rms_norm(x, g) @ w_up
    return jnp.square(jax.nn.relu(u)) @ w_down


def setup_inputs(seed: int = 0) -> dict:
    key = jax.random.key(seed)
    keys = iter(jax.random.split(key, 64))

    def nrm(shape, scale=1.0):
        return scale * jax.random.normal(next(keys), shape, F32)

    def gain(shape):
        return 1.0 + 0.05 * nrm(shape)

    n_pages = PAST_LEN // PAGE_SIZE
    n_pool = (5 * DEC_BATCH * n_pages) // 4
    win_buf = min(WINDOW, PAST_LEN)
    inp = {}
    inp['x_prompt'] = nrm((BATCH, SEQ, D_MODEL))
    inp['x_sample'] = nrm((DEC_BATCH, DEC_SEQ, D_MODEL))
    inp['cache_mla'] = nrm((N_AB, n_pool, PAGE_SIZE, MLA_KV_LORA + MLA_ROPE))
    inp['cache_fox_kv'] = nrm((N_AB, n_pool, PAGE_SIZE, 2, FOX_KV_HEADS, HEAD_DIM))
    inp['cache_fox_logf'] = jax.nn.log_sigmoid(2.0 + 0.5 * nrm((N_AB, n_pool, PAGE_SIZE, FOX_HEADS)))
    inp['cache_nsa_cmp_kv'] = nrm((N_CD, n_pool, PAGE_SIZE, 2, NSA_KV_HEADS, HEAD_DIM))
    inp['cache_nsa_sel_kv'] = nrm((N_CD, n_pool, PAGE_SIZE, 2, NSA_KV_HEADS, HEAD_DIM))
    inp['state_nsa_win_kv'] = nrm((N_CD, DEC_BATCH, win_buf, 2, NSA_KV_HEADS, HEAD_DIM))
    inp['cache_moba_kv'] = nrm((N_CD, n_pool, PAGE_SIZE, 2, MOBA_KV_HEADS, HEAD_DIM))
    perm = jax.random.permutation(next(keys), n_pool)
    inp['page_table'] = perm[:DEC_BATCH * n_pages].reshape(DEC_BATCH, n_pages).astype(jnp.int32)
    inp['g_mix'] = gain((DEPTH, D_MODEL))
    inp['g_ffn'] = gain((DEPTH, D_MODEL))
    inp['w_ffn_up'] = nrm((DEPTH, D_MODEL, D_FF), D_MODEL ** -0.5)
    inp['w_ffn_down'] = nrm((DEPTH, D_FF, D_MODEL), 0.5 * D_FF ** -0.5)
    inp['w_in_ab'] = nrm((N_AB, D_MODEL, AB_IN), D_MODEL ** -0.5)
    inp['w_out_ab'] = nrm((N_AB, AB_OUT, D_MODEL), AB_OUT ** -0.5)
    inp['g_mla_q_lora'] = gain((N_AB, MLA_Q_LORA))
    inp['w_mla_q_up'] = nrm((N_AB, MLA_Q_LORA, MLA_HEADS * (MLA_NOPE + MLA_ROPE)), MLA_Q_LORA ** -0.5)
    inp['g_mla_q_nope'] = gain((N_AB, MLA_NOPE))
    inp['g_mla_q_rope'] = gain((N_AB, MLA_ROPE))
    inp['g_mla_kv_lora'] = gain((N_AB, MLA_KV_LORA))
    inp['g_mla_k_rope'] = gain((N_AB, MLA_ROPE))
    inp['w_mla_kv_up'] = nrm((N_AB, MLA_KV_LORA, MLA_HEADS * (MLA_NOPE + MLA_V)), MLA_KV_LORA ** -0.5)
    inp['g_mla_k_nope'] = gain((N_AB, MLA_NOPE))
    inp['g_fox_q'] = gain((N_AB, HEAD_DIM))
    inp['g_fox_k'] = gain((N_AB, HEAD_DIM))
    inp['b_fox_f'] = 2.0 + 0.5 * nrm((N_AB, FOX_HEADS))
    inp['w_in_cd'] = nrm((N_CD, D_MODEL, CD_IN), D_MODEL ** -0.5)
    inp['w_out_cd'] = nrm((N_CD, CD_OUT, D_MODEL), CD_OUT ** -0.5)
    inp['g_nsa_q'] = gain((N_CD, HEAD_DIM))
    inp['g_nsa_kc'] = gain((N_CD, HEAD_DIM))
    inp['g_nsa_ks'] = gain((N_CD, HEAD_DIM))
    inp['g_nsa_kw'] = gain((N_CD, HEAD_DIM))
    inp['nsa_pe_k'] = nrm((N_CD, CMP_LEN, HEAD_DIM), 0.5)
    inp['nsa_pe_v'] = nrm((N_CD, CMP_LEN, HEAD_DIM), 0.5)
    inp['w_nsa_ck1'] = nrm((N_CD, CMP_LEN * HEAD_DIM, CMP_HIDDEN), (CMP_LEN * HEAD_DIM) ** -0.5)
    inp['w_nsa_ck2'] = nrm((N_CD, CMP_HIDDEN, HEAD_DIM), 2.0 * CMP_HIDDEN ** -0.5)
    inp['w_nsa_cv1'] = nrm((N_CD, CMP_LEN * HEAD_DIM, CMP_HIDDEN), (CMP_LEN * HEAD_DIM) ** -0.5)
    inp['w_nsa_cv2'] = nrm((N_CD, CMP_HIDDEN, HEAD_DIM), 2.0 * CMP_HIDDEN ** -0.5)
    inp['b_nsa_gate'] = nrm((N_CD, 3 * NSA_HEADS), 0.1)
    inp['g_moba_q'] = gain((N_CD, HEAD_DIM))
    inp['g_moba_k'] = gain((N_CD, HEAD_DIM))
    return inp


def reference(x_prompt, x_sample, cache_mla, cache_fox_kv, cache_fox_logf, cache_nsa_cmp_kv, cache_nsa_sel_kv,
              state_nsa_win_kv, cache_moba_kv, page_table, g_mix, g_ffn, w_ffn_up, w_ffn_down, w_in_ab, w_out_ab,
              g_mla_q_lora, w_mla_q_up, g_mla_q_nope, g_mla_q_rope, g_mla_kv_lora, g_mla_k_rope, w_mla_kv_up,
              g_mla_k_nope, g_fox_q, g_fox_k, b_fox_f, w_in_cd, w_out_cd, g_nsa_q, g_nsa_kc, g_nsa_ks, g_nsa_kw,
              nsa_pe_k, nsa_pe_v, w_nsa_ck1, w_nsa_ck2, w_nsa_cv1, w_nsa_cv2, b_nsa_gate, g_moba_q, g_moba_k):
    n_alibi = NSA_HEADS + MOBA_HEADS
    slopes = jnp.exp2(-8.0 * (jnp.arange(n_alibi, dtype=F32) + 1.0) / n_alibi)
    slopes_nsa, slopes_moba = slopes[0::2], slopes[1::2]
    xp, xs = x_prompt, x_sample
    mla_p, mla_s, fkv_p, fkv_s, flf_p, flf_s = [], [], [], [], [], []
    cmp_p, cmp_s, sel_p, sel_s, win_p, win_s, mb_p, mb_s = [], [], [], [], [], [], [], []
    for layer in range(DEPTH):
        hp = rms_norm(xp, g_mix[layer])
        hs = rms_norm(xs, g_mix[layer])
        i = layer // 2
        if layer % 2 == 0:
            pa = dict(w_in=w_in_ab[i], w_out=w_out_ab[i], g_q_lora=g_mla_q_lora[i], w_q_up=w_mla_q_up[i],
                      g_q_nope=g_mla_q_nope[i], g_q_rope=g_mla_q_rope[i], g_kv_lora=g_mla_kv_lora[i],
                      g_k_rope=g_mla_k_rope[i], w_kv_up=w_mla_kv_up[i], g_k_nope=g_mla_k_nope[i],
                      g_fox_q=g_fox_q[i], g_fox_k=g_fox_k[i], b_fox_f=b_fox_f[i])
            yp, a1, a2, a3 = ab_prompt(hp, pa)
            ys, b1, b2, b3 = ab_sample(hs, pa, i, cache_mla, cache_fox_kv, cache_fox_logf, page_table)
            mla_p.append(a1); fkv_p.append(a2); flf_p.append(a3)
            mla_s.append(b1); fkv_s.append(b2); flf_s.append(b3)
        else:
            pc = dict(w_in=w_in_cd[i], w_out=w_out_cd[i], g_nsa_q=g_nsa_q[i], g_nsa_kc=g_nsa_kc[i],
                      g_nsa_ks=g_nsa_ks[i], g_nsa_kw=g_nsa_kw[i], pe_k=nsa_pe_k[i], pe_v=nsa_pe_v[i],
                      w_ck1=w_nsa_ck1[i], w_ck2=w_nsa_ck2[i], w_cv1=w_nsa_cv1[i], w_cv2=w_nsa_cv2[i],
                      b_gate=b_nsa_gate[i], g_moba_q=g_moba_q[i], g_moba_k=g_moba_k[i])
            yp, a1, a2, a3, a4 = cd_prompt(hp, pc, slopes_nsa, slopes_moba)
            ys, b1, b2, b3, b4 = cd_sample(hs, pc, i, cache_nsa_cmp_kv, cache_nsa_sel_kv, state_nsa_win_kv,
                                           cache_moba_kv, page_table, slopes_nsa, slopes_moba)
            cmp_p.append(a1); sel_p.append(a2); win_p.append(a3); mb_p.append(a4)
            cmp_s.append(b1); sel_s.append(b2); win_s.append(b3); mb_s.append(b4)
        xp = xp + yp
        xs = xs + ys
        xp = xp + sq_relu_mlp(xp, g_ffn[layer], w_ffn_up[layer], w_ffn_down[layer])
        xs = xs + sq_relu_mlp(xs, g_ffn[layer], w_ffn_up[layer], w_ffn_down[layer])
    y_prompt, y_sample = xp, xs
    mla_prompt, mla_sample = jnp.stack(mla_p), jnp.stack(mla_s)
    fox_kv_prompt, fox_kv_sample = jnp.stack(fkv_p), jnp.stack(fkv_s)
    fox_logf_prompt, fox_logf_sample = jnp.stack(flf_p), jnp.stack(flf_s)
    nsa_cmp_prompt, nsa_cmp_sample = jnp.stack(cmp_p), jnp.stack(cmp_s)
    nsa_sel_prompt, nsa_sel_sample = jnp.stack(sel_p), jnp.stack(sel_s)
    nsa_win_prompt, nsa_win_sample = jnp.stack(win_p), jnp.stack(win_s)
    moba_prompt, moba_sample = jnp.stack(mb_p), jnp.stack(mb_s)
    return (y_prompt, y_sample, mla_prompt, mla_sample, fox_kv_prompt, fox_kv_sample, fox_logf_prompt,
            fox_logf_sample, nsa_cmp_prompt, nsa_cmp_sample, nsa_sel_prompt, nsa_sel_sample, nsa_win_prompt,
            nsa_win_sample, moba_prompt, moba_sample)
```

```python
import functools
import math

import numpy as np
import jax
import jax.numpy as jnp
from jax import lax
from jax.experimental import pallas as pl
from jax.experimental.pallas import tpu as pltpu

F32 = jnp.float32
BF16 = jnp.bfloat16

HEAD_DIM = 64
N_HEADS = 8
N_KV = 2
GROUP = N_HEADS // N_KV
MLA_Q_LORA = 256
MLA_KV_LORA = 128
MLA_NOPE = 64
MLA_ROPE = 32
MLA_V = 64
ROPE_THETA = 10000.0
CMP_LEN = 32
CMP_STRIDE = 16
CMP_HIDDEN = 128
SEL_BLOCK = 64
SEL_TOPN = 16
WINDOW = 512
MOBA_BLOCK = 256
MOBA_TOPK = 3
NORM_EPS = 1e-6
PAGE = 128

LANE = 128
NEG = -1e30
BLOCK_NEG = -(2.0 ** 100)
VMEM_LIMIT = 56 * 1024 * 1024

MLA_SCALE = 1.0 / math.sqrt(MLA_NOPE + MLA_ROPE)
HD_SCALE = 1.0 / math.sqrt(HEAD_DIM)


def _dot(a, b):
    return jnp.dot(a, b, preferred_element_type=F32)


def _dot_nt(a, b):
    return lax.dot_general(a, b, (((1,), (1,)), ((), ())), preferred_element_type=F32)


def _split2(x):
    hi = x.astype(BF16)
    lo = (x - hi.astype(F32)).astype(BF16)
    return hi, lo


def _iota(shape, axis):
    return lax.broadcasted_iota(jnp.int32, shape, axis)


def _rms(x, g):
    ms = jnp.mean(x * x, axis=-1, keepdims=True)
    return x * lax.rsqrt(ms + NORM_EPS) * g


def _seg_norm(z, g, split, n_a, n_b=0):
    sq = z * z
    if split >= LANE:
        sa = jnp.sum(sq, axis=-1, keepdims=True)
        inv = lax.rsqrt(sa * (1.0 / n_a) + NORM_EPS)
    else:
        in_a = _iota(z.shape, 1) < split
        sa = jnp.sum(jnp.where(in_a, sq, 0.0), axis=-1, keepdims=True)
        sb = jnp.sum(jnp.where(in_a, 0.0, sq), axis=-1, keepdims=True)
        inv = jnp.where(in_a, lax.rsqrt(sa * (1.0 / n_a) + NORM_EPS),
                        lax.rsqrt(sb * (1.0 / n_b) + NORM_EPS))
    return z * inv * g


def _rope_slot(x, cos, sin):
    lane = _iota(x.shape, 1)
    rot = jnp.where(lane < 80, pltpu.roll(x, LANE - 16, 1), pltpu.roll(x, 16, 1))
    return x * cos + rot * sin


def _cumsum_lanes(x):
    lane = _iota(x.shape, 1)
    sh = 1
    while sh < x.shape[1]:
        x = x + jnp.where(lane >= sh, pltpu.roll(x, sh, 1), 0.0)
        sh *= 2
    return x


def _topk_select(sc, k):
    lane = _iota(sc.shape, 1).astype(F32)
    sel = jnp.zeros(sc.shape, F32)
    for _ in range(k):
        mx = jnp.max(sc, axis=-1, keepdims=True)
        first = jnp.min(jnp.where(sc == mx, lane, 1e9), axis=-1, keepdims=True)
        pick = (lane == first) & (mx > 0.5 * NEG)
        sel = jnp.where(pick, 1.0, sel)
        sc = jnp.where(pick, 3.0 * NEG, sc)
    return sel


def _online_update(s, v_list, m_ref, l_ref, acc_ref, idx=None):
    if idx is None:
        m_prev, l_prev, a_prev = m_ref[...], l_ref[...], acc_ref[...]
    else:
        m_prev, l_prev, a_prev = m_ref[idx], l_ref[idx], acc_ref[idx]
    m_new = jnp.maximum(m_prev, jnp.max(s, axis=-1, keepdims=True))
    alpha = jnp.exp(m_prev - m_new)
    p = jnp.exp(s - m_new)
    l_new = alpha * l_prev + jnp.sum(p, axis=-1, keepdims=True)
    pb = p.astype(BF16)
    w = s.shape[1] // len(v_list)
    pv = _dot(pb[:, 0:w], v_list[0])
    for n in range(1, len(v_list)):
        pv = pv + _dot(pb[:, n * w:(n + 1) * w], v_list[n])
    a_new = alpha * a_prev + pv
    if idx is None:
        m_ref[...], l_ref[...], acc_ref[...] = m_new, l_new, a_new
    else:
        m_ref[idx], l_ref[idx], acc_ref[idx] = m_new, l_new, a_new


def _params(sem, vmem=VMEM_LIMIT):
    return pltpu.CompilerParams(dimension_semantics=sem, vmem_limit_bytes=vmem)


def _full(shape):
    nd = len(shape)
    return pl.BlockSpec(shape, lambda *_: (0,) * nd)


def _alibi_slopes():
    n = 2 * N_HEADS
    s = np.exp2(-8.0 * (np.arange(n, dtype=np.float32) + 1.0) / n).astype(np.float32)
    return s[0::2], s[1::2]


def _take_cols(w, idx):
    idx = np.asarray(idx)
    wz = jnp.concatenate([w, jnp.zeros((w.shape[0], 1), w.dtype)], axis=1)
    return wz[:, np.where(idx < 0, w.shape[1], idx)]


def _slots_idx(base, n_heads, width, lane_off_fn):
    idx = -np.ones((n_heads * LANE,), np.int64)
    for h in range(n_heads):
        o = lane_off_fn(h)
        idx[h * LANE + o:h * LANE + o + width] = base + h * width + np.arange(width)
    return idx


def _ab_col_index():
    c_q = np.arange(0, 256)
    c_kv = np.arange(256, 384)
    kpe = -np.ones((LANE,), np.int64)
    kpe[64:96] = np.arange(384, 416)
    fq = _slots_idx(416, N_HEADS, HEAD_DIM, lambda h: HEAD_DIM * (h // GROUP))
    fkv = np.arange(928, 1184)
    ff = -np.ones((LANE,), np.int64)
    ff[0:8] = np.arange(1184, 1192)
    return np.concatenate([c_q, c_kv, kpe, fq, fkv, ff])


def _cd_col_index():
    nq = _slots_idx(0, N_HEADS, HEAD_DIM, lambda h: HEAD_DIM * (h // GROUP))
    kv3 = np.arange(512, 1280)
    gl = -np.ones((LANE,), np.int64)
    gl[0:24] = np.arange(1280, 1304)
    mq = _slots_idx(1304, N_HEADS, HEAD_DIM, lambda h: HEAD_DIM * (h // GROUP))
    mkv = np.arange(1816, 2072)
    return np.concatenate([nq, kv3, gl, mq, mkv])


def _pad_lanes(v, off, total=LANE):
    v = v.reshape(1, -1).astype(F32)
    return jnp.pad(v, ((0, 0), (off, total - off - v.shape[1])))


def _rope_tables(pos):
    half = MLA_ROPE // 2
    inv = jnp.power(ROPE_THETA, -jnp.arange(half, dtype=F32) / half)
    ang = pos.astype(F32)[:, None] * inv[None, :]
    cos, sin = jnp.cos(ang), jnp.sin(ang)
    n = pos.shape[0]
    ones = jnp.ones((n, 64), F32)
    ctab = jnp.concatenate([ones, cos, cos, jnp.ones((n, 32), F32)], axis=1)
    stab = jnp.concatenate([jnp.zeros((n, 64), F32), -sin, sin, jnp.zeros((n, 32), F32)], axis=1)
    return ctab, stab


def _compress_weights(w1, w2):
    w1r = w1.reshape(2, CMP_STRIDE, HEAD_DIM, CMP_HIDDEN)
    z = jnp.zeros_like(w1r)
    h0 = jnp.concatenate([w1r, z], axis=-1)
    h1 = jnp.concatenate([z, w1r], axis=-1)
    big = jnp.stack([h0, h1], axis=2)
    big = big.reshape(2, CMP_STRIDE * 2 * HEAD_DIM, 2 * CMP_HIDDEN).astype(BF16)
    z2 = jnp.zeros_like(w2)
    w2b = jnp.concatenate([jnp.concatenate([w2, z2], axis=1),
                           jnp.concatenate([z2, w2], axis=1)], axis=0).astype(BF16)
    return big[0], big[1], w2b


def _compress_pe(pe):
    t = jnp.tile(pe.reshape(2, CMP_STRIDE, 1, HEAD_DIM), (1, 1, 2, 1))
    t = t.reshape(2, 1, CMP_STRIDE * 2 * HEAD_DIM).astype(F32)
    return t[0], t[1]


def _proj_ab_body(x_ref, gmix_ref, win_ref, gql_ref, wqup_ref, gq_ref, gkpe_ref, gkvl_ref,
                  wk_ref, wv_ref, gkn_ref, gfq_ref, gfk_ref, bf_ref, cos_ref, sin_ref,
                  q_out, lat_out, mk_out, mv_out, fq_out, fkv_out, fkvb_out, lf_out):
    h = _rms(x_ref[...], gmix_ref[...]).astype(BF16)
    z = _dot(h, win_ref[...])
    cos, sin = cos_ref[...], sin_ref[...]
    cqn = _rms(z[:, 0:256], gql_ref[...]).astype(BF16)
    qz = _dot(cqn, wqup_ref[...])
    gq = gq_ref[...]
    for hh in range(N_HEADS):
        slot = _seg_norm(qz[:, hh * LANE:(hh + 1) * LANE], gq, 64, MLA_NOPE, MLA_ROPE)
        q_out[:, hh * LANE:(hh + 1) * LANE] = (_rope_slot(slot, cos, sin) * MLA_SCALE).astype(BF16)
    ckvn = _rms(z[:, 256:384], gkvl_ref[...])
    krot = _rope_slot(_seg_norm(z[:, 384:512], gkpe_ref[...], LANE, MLA_ROPE), cos, sin)
    lat_out[:, 0:MLA_KV_LORA] = ckvn
    lat_out[:, MLA_KV_LORA:MLA_KV_LORA + MLA_ROPE] = pltpu.roll(krot, 64, 1)[:, 0:MLA_ROPE]
    cb = ckvn.astype(BF16)
    kraw = _dot(cb, wk_ref[...])
    gkn = gkn_ref[...]
    for hh in range(N_HEADS):
        slot = _seg_norm(kraw[:, hh * LANE:(hh + 1) * LANE], gkn, LANE, MLA_NOPE) + krot
        mk_out[:, hh * LANE:(hh + 1) * LANE] = slot.astype(BF16)
    mv_out[...] = _dot(cb, wv_ref[...]).astype(BF16)
    gfq = gfq_ref[...]
    for hh in range(N_HEADS):
        slot = _seg_norm(z[:, 512 + hh * LANE:512 + (hh + 1) * LANE], gfq, LANE, HEAD_DIM)
        fq_out[:, hh * LANE:(hh + 1) * LANE] = (slot * HD_SCALE).astype(BF16)
    kpair = _seg_norm(z[:, 1536:1664], gfk_ref[...], 64, HEAD_DIM, HEAD_DIM)
    vpair = z[:, 1664:1792]
    fkv_out[:, 0:LANE] = kpair
    fkv_out[:, LANE:2 * LANE] = vpair
    fkvb_out[:, 0:LANE] = kpair.astype(BF16)
    fkvb_out[:, LANE:2 * LANE] = vpair.astype(BF16)
    ff = z[:, 1792:1920] + bf_ref[...]
    ls = jnp.minimum(ff, 0.0) - jnp.log1p(jnp.exp(-jnp.abs(ff)))
    lf_out[...] = ls[:, 0:N_HEADS]


def _proj_ab(x2, pos_tabs, w, tm):
    n, d = x2.shape
    cos, sin = pos_tabs
    ntab = cos.shape[0] // tm
    row = lambda width: pl.BlockSpec((tm, width), lambda i: (i, 0))
    tab = pl.BlockSpec((tm, LANE), lambda i: (i % ntab, 0))
    ins = [x2, w['g_mix'], w['w_in'], w['g_q_lora'], w['w_q_up'], w['g_q'], w['g_kpe'], w['g_kv_lora'],
           w['w_k'], w['w_v'], w['g_k_nope'], w['g_fox_q'], w['g_fox_k'], w['b_fox_f'], cos, sin]
    in_specs = [row(d)] + [_full(a.shape) for a in ins[1:14]] + [tab, tab]
    out_shape = [jax.ShapeDtypeStruct((n, 1024), BF16), jax.ShapeDtypeStruct((n, 160), F32),
                 jax.ShapeDtypeStruct((n, 1024), BF16), jax.ShapeDtypeStruct((n, 512), BF16),
                 jax.ShapeDtypeStruct((n, 1024), BF16), jax.ShapeDtypeStruct((n, 256), F32),
                 jax.ShapeDtypeStruct((n, 256), BF16), jax.ShapeDtypeStruct((n, N_HEADS), F32)]
    out_specs = [row(s.shape[1]) for s in out_shape]
    return pl.pallas_call(_proj_ab_body, grid=(n // tm,), in_specs=in_specs, out_specs=out_specs,
                          out_shape=out_shape, compiler_params=_params(("parallel",)),
                          name="proj_ab")(*ins)


def _proj_cd_body(with_ksum, tm, x_ref, gmix_ref, win_ref, gnq_ref, gks_ref, gkw_ref, bg_ref, gmq_ref, gmk_ref,
                  nq_out, cmp_out, sel_out, selb_out, win_out, winb_out, gate_out, mq_out, mqf_out,
                  mkv_out, mkvb_out, *maybe_ksum):
    h = _rms(x_ref[...], gmix_ref[...]).astype(BF16)
    z = _dot(h, win_ref[...])
    gnq = gnq_ref[...]
    for hh in range(N_HEADS):
        slot = _seg_norm(z[:, hh * LANE:(hh + 1) * LANE], gnq, LANE, HEAD_DIM)
        nq_out[:, hh * LANE:(hh + 1) * LANE] = (slot * HD_SCALE).astype(BF16)
    cmp_out[...] = z[:, 1024:1280]

    def kv_out(base, g_ref, f_ref, b_ref):
        kp = _seg_norm(z[:, base:base + LANE], g_ref[...], 64, HEAD_DIM, HEAD_DIM)
        vp = z[:, base + LANE:base + 2 * LANE]
        f_ref[:, 0:LANE] = kp
        f_ref[:, LANE:2 * LANE] = vp
        b_ref[:, 0:LANE] = kp.astype(BF16)
        b_ref[:, LANE:2 * LANE] = vp.astype(BF16)
        return kp

    kv_out(1280, gks_ref, sel_out, selb_out)
    kv_out(1536, gkw_ref, win_out, winb_out)
    gl = z[:, 1792:1920] + bg_ref[...]
    gate_out[...] = (1.0 / (1.0 + jnp.exp(-gl)))[:, 0:3 * N_HEADS]
    gmq = gmq_ref[...]
    for hh in range(N_HEADS):
        slot = _seg_norm(z[:, 1920 + hh * LANE:1920 + (hh + 1) * LANE], gmq, LANE, HEAD_DIM)
        mqf_out[:, hh * LANE:(hh + 1) * LANE] = slot
        mq_out[:, hh * LANE:(hh + 1) * LANE] = (slot * HD_SCALE).astype(BF16)
    kp = kv_out(2944, gmk_ref, mkv_out, mkvb_out)
    if with_ksum:
        ksum_out = maybe_ksum[0]
        for blk in range(tm // MOBA_BLOCK):
            ksum_out[blk] = jnp.sum(kp[blk * MOBA_BLOCK:(blk + 1) * MOBA_BLOCK], axis=0, keepdims=True) \
                * (1.0 / MOBA_BLOCK)


def _proj_cd(x2, w, tm, with_ksum):
    n, d = x2.shape
    row = lambda width: pl.BlockSpec((tm, width), lambda i: (i, 0))
    ins = [x2, w['g_mix'], w['w_in'], w['g_nsa_q'], w['g_nsa_ks'], w['g_nsa_kw'], w['b_gate'],
           w['g_moba_q'], w['g_moba_k']]
    in_specs = [row(d)] + [_full(a.shape) for a in ins[1:]]
    out_shape = [jax.ShapeDtypeStruct((n, 1024), BF16), jax.ShapeDtypeStruct((n, 256), F32),
                 jax.ShapeDtypeStruct((n, 256), F32), jax.ShapeDtypeStruct((n, 256), BF16),
                 jax.ShapeDtypeStruct((n, 256), F32), jax.ShapeDtypeStruct((n, 256), BF16),
                 jax.ShapeDtypeStruct((n, 3 * N_HEADS), F32), jax.ShapeDtypeStruct((n, 1024), BF16),
                 jax.ShapeDtypeStruct((n, 1024), F32), jax.ShapeDtypeStruct((n, 256), F32),
                 jax.ShapeDtypeStruct((n, 256), BF16)]
    out_specs = [row(s.shape[1]) for s in out_shape]
    if with_ksum:
        nb = tm // MOBA_BLOCK
        out_shape.append(jax.ShapeDtypeStruct((n // MOBA_BLOCK, 1, LANE), F32))
        out_specs.append(pl.BlockSpec((nb, 1, LANE), lambda i: (i, 0, 0)))
    return pl.pallas_call(functools.partial(_proj_cd_body, with_ksum, tm), grid=(n // tm,),
                          in_specs=in_specs, out_specs=out_specs, out_shape=out_shape,
                          compiler_params=_params(("parallel",)), name="proj_cd")(*ins)


def _post_body(n_a, x_ref, *refs):
    a_refs = refs[:n_a]
    b_ref, woa_ref, wob_ref, g_ref, wup_ref, wdn_ref, o_ref, x1_ref, h_ref, acc_ref = refs[n_a:]
    f = pl.program_id(1)

    @pl.when(f == 0)
    def _():
        a = a_refs[0][...]
        for r in a_refs[1:]:
            a = a + r[...]
        x1 = x_ref[...] + _dot(a.astype(BF16), woa_ref[...]) + _dot(b_ref[...].astype(BF16), wob_ref[...])
        x1_ref[...] = x1
        h_ref[...] = _rms(x1, g_ref[...]).astype(BF16)
        acc_ref[...] = jnp.zeros_like(acc_ref)

    u = _dot(h_ref[...], wup_ref[...])
    r = jnp.maximum(u, 0.0)
    acc_ref[...] += _dot((r * r).astype(BF16), wdn_ref[...])

    @pl.when(f == pl.num_programs(1) - 1)
    def _():
        o_ref[...] = x1_ref[...] + acc_ref[...]


def _post(x2, a_list, b, w_out, g_ffn, w_up, w_dn, tm):
    n, d = x2.shape
    dff = w_up.shape[1]
    tf = min(1024, dff)
    half = w_out.shape[0] // 2
    row = lambda width: pl.BlockSpec((tm, width), lambda i, f: (i, 0))
    in_specs = ([row(d)] + [row(half)] * (len(a_list) + 1)
                + [pl.BlockSpec((half, d), lambda i, f: (0, 0)), pl.BlockSpec((half, d), lambda i, f: (1, 0)),
                   pl.BlockSpec((1, d), lambda i, f: (0, 0)),
                   pl.BlockSpec((d, tf), lambda i, f: (0, f)), pl.BlockSpec((tf, d), lambda i, f: (f, 0))])
    return pl.pallas_call(
        functools.partial(_post_body, len(a_list)), grid=(n // tm, dff // tf), in_specs=in_specs,
        out_specs=row(d), out_shape=jax.ShapeDtypeStruct((n, d), F32),
        scratch_shapes=[pltpu.VMEM((tm, d), F32), pltpu.VMEM((tm, d), BF16), pltpu.VMEM((tm, d), F32)],
        compiler_params=_params(("parallel", "arbitrary")), name="post_mlp",
    )(x2, *a_list, b, w_out, w_out, g_ffn, w_up, w_dn)


def _cumsum_body(n_chunks, lf_ref, o_ref):
    def step(c, carry):
        off = pl.multiple_of(c * LANE, LANE)
        cum = _cumsum_lanes(lf_ref[:, pl.ds(off, LANE)]) + carry
        o_ref[:, pl.ds(off, LANE)] = cum
        return cum[:, LANE - 1:LANE]
    lax.fori_loop(0, n_chunks, step, jnp.zeros((N_HEADS, 1), F32))


def _cumsum_seq(lft):
    b, h, s = lft.shape
    spec = pl.BlockSpec((None, h, s), lambda i: (i, 0, 0))
    return pl.pallas_call(functools.partial(_cumsum_body, s // LANE), grid=(b,), in_specs=[spec],
                          out_specs=spec, out_shape=jax.ShapeDtypeStruct((b, h, s), F32),
                          compiler_params=_params(("parallel",)), name="fox_cumsum")(lft)


def _flash_body(kind, t, nkv, slopes, gate_idx, *refs):
    refs = list(refs)
    q_ref, k_ref, v_ref = refs[0:3]
    pos = 3
    cq_ref = ck_ref = e_ref = sb_ref = gate_ref = None
    if kind == 'fox':
        cq_ref, ck_ref = refs[pos:pos + 2]
        pos += 2
    if kind in ('sel', 'moba'):
        e_ref, sb_ref = refs[pos:pos + 2]
        pos += 2
    if gate_idx is not None:
        gate_ref = refs[pos]
        pos += 1
    o_ref, m_ref, l_ref, acc_ref = refs[pos:pos + 4]
    i = pl.program_id(1)
    jj = pl.program_id(2)
    banded = kind == 'win'
    j = i - (nkv - 1) + jj if banded else jj

    @pl.when(jj == 0)
    def _():
        m_ref[...] = jnp.full(m_ref.shape, NEG, F32)
        l_ref[...] = jnp.zeros(l_ref.shape, F32)
        acc_ref[...] = jnp.zeros(acc_ref.shape, F32)

    def step(masked):
        dist = (i * t + _iota((t, t), 0)) - (j * t + _iota((t, t), 1))
        distf = dist.astype(F32) if slopes is not None else None
        if masked:
            mask = (dist >= 0) & (dist < WINDOW) if banded else dist >= 0
        for hh in range(N_HEADS):
            qh = q_ref[:, hh * LANE:(hh + 1) * LANE]
            if kind == 'mla':
                kh = k_ref[:, hh * LANE:(hh + 1) * LANE]
                vh = v_ref[:, (hh // 2) * LANE:(hh // 2 + 1) * LANE]
            else:
                kh = k_ref[...]
                vh = v_ref[...]
            if kind == 'sel':
                kvh = hh // GROUP
                qh = jnp.concatenate([qh, sb_ref[:, kvh * LANE:(kvh + 1) * LANE]], axis=1)
                kh = jnp.concatenate([kh, e_ref[...]], axis=1)
            if kind == 'moba':
                qh = jnp.concatenate([qh, sb_ref[:, hh * LANE:(hh + 1) * LANE]], axis=1)
                kh = jnp.concatenate([kh, e_ref[...]], axis=1)
            s = _dot_nt(qh, kh)
            if kind == 'fox':
                s = s + (cq_ref[:, hh:hh + 1] - ck_ref[hh:hh + 1, :])
            if slopes is not None:
                s = s - float(slopes[hh]) * distf
            if masked:
                s = jnp.where(mask, s, NEG)
            _online_update(s, [vh], m_ref, l_ref, acc_ref, idx=hh)

    if banded:
        pl.when(j >= 0)(lambda: step(True))
    else:
        pl.when(j < i)(lambda: step(False))
        pl.when(j == i)(lambda: step(True))

    @pl.when(j == i)
    def _():
        for hh in range(N_HEADS):
            o = acc_ref[hh] / l_ref[hh]
            half = (hh % 2) if kind == 'mla' else (hh // GROUP)
            oh = o[:, half * HEAD_DIM:(half + 1) * HEAD_DIM]
            if gate_ref is not None:
                oh = oh * gate_ref[:, 3 * hh + gate_idx:3 * hh + gate_idx + 1]
            o_ref[:, hh * HEAD_DIM:(hh + 1) * HEAD_DIM] = oh


def _flash(kind, q, k_arr, v_arr, t, extra=(), gates=None, gate_idx=None, slopes=None):
    b, s, _ = q.shape
    nq = s // t
    banded = kind == 'win'
    nkv = ((WINDOW + t - 2) // t + 1) if banded else nq
    if banded:
        kvi = lambda i, jj: jnp.maximum(i - (nkv - 1) + jj, 0)
    else:
        kvi = lambda i, jj: jnp.minimum(jj, i)
    qspec = pl.BlockSpec((None, t, 1024), lambda bb, i, jj: (bb, i, 0))
    if kind == 'mla':
        kspec = pl.BlockSpec((None, t, 1024), lambda bb, i, jj: (bb, kvi(i, jj), 0))
        vspec = pl.BlockSpec((None, t, 512), lambda bb, i, jj: (bb, kvi(i, jj), 0))
    else:
        kspec = pl.BlockSpec((None, t, LANE), lambda bb, i, jj: (bb, kvi(i, jj), 0))
        vspec = pl.BlockSpec((None, t, LANE), lambda bb, i, jj: (bb, kvi(i, jj), 1))
    ins = [q, k_arr, v_arr]
    in_specs = [qspec, kspec, vspec]
    if kind == 'fox':
        cq, ckt = extra
        ins += [cq, ckt]
        in_specs += [pl.BlockSpec((None, t, N_HEADS), lambda bb, i, jj: (bb, i, 0)),
                     pl.BlockSpec((None, N_HEADS, t), lambda bb, i, jj: (bb, 0, kvi(i, jj)))]
    if kind in ('sel', 'moba'):
        e_tab, selb = extra
        ins += [e_tab, selb]
        in_specs += [pl.BlockSpec((t, LANE), lambda bb, i, jj: (kvi(i, jj), 0)),
                     pl.BlockSpec((None, t, selb.shape[2]), lambda bb, i, jj: (bb, i, 0))]
    if gates is not None:
        ins.append(gates)
        in_specs.append(pl.BlockSpec((None, t, 3 * N_HEADS), lambda bb, i, jj: (bb, i, 0)))
    return pl.pallas_call(
        functools.partial(_flash_body, kind, t, nkv, slopes, gate_idx),
        grid=(b, nq, nkv), in_specs=in_specs,
        out_specs=pl.BlockSpec((None, t, 512), lambda bb, i, jj: (bb, i, 0)),
        out_shape=jax.ShapeDtypeStruct((b, s, 512), F32),
        scratch_shapes=[pltpu.VMEM((N_HEADS, t, 1), F32), pltpu.VMEM((N_HEADS, t, 1), F32),
                        pltpu.VMEM((N_HEADS, t, LANE), F32)],
        compiler_params=_params(("parallel", "parallel", "arbitrary")), name="flash_" + kind,
    )(*ins)


def _compress_body(n_blk, *refs):
    blk_refs = refs[:n_blk]
    (wka_ref, wkb_ref, wva_ref, wvb_ref, w2k_ref, w2v_ref, peka_ref, pekb_ref, peva_ref, pevb_ref,
     gkc_ref, kc_out, vc_out, ck_ref, cv_ref) = refs[n_blk:]
    j = pl.program_id(1)

    @pl.when(j == 0)
    def _():
        ck_ref[...] = jnp.zeros_like(ck_ref)
        cv_ref[...] = jnp.zeros_like(cv_ref)

    if n_blk == 1:
        x = blk_refs[0][...]
    else:
        x = jnp.concatenate([r[...] for r in blk_refs], axis=0)
    n = x.shape[0]
    row = _iota((n, 2 * CMP_HIDDEN), 0)

    def stream(off, wa_ref, wb_ref, pea_ref, peb_ref, w2_ref, carry_ref):
        xs = jnp.concatenate([x[:, r * 256 + off:r * 256 + off + LANE] for r in range(CMP_STRIDE)], axis=1)
        a = _dot((xs + pea_ref[...]).astype(BF16), wa_ref[...])
        bpart = _dot((xs + peb_ref[...]).astype(BF16), wb_ref[...])
        a_prev = jnp.where(row == 0, carry_ref[...], pltpu.roll(a, 1, 0))
        carry_ref[...] = a[n - 1:n, :]
        hid = a_prev + bpart
        act = hid * (1.0 / (1.0 + jnp.exp(-hid)))
        return _dot(act.astype(BF16), w2_ref[...])

    kc = stream(0, wka_ref, wkb_ref, peka_ref, pekb_ref, w2k_ref, ck_ref)
    kc_out[...] = _seg_norm(kc, gkc_ref[...], 64, HEAD_DIM, HEAD_DIM).astype(BF16)
    vc_out[...] = stream(LANE, wva_ref, wvb_ref, peva_ref, pevb_ref, w2v_ref, cv_ref).astype(BF16)


def _compress_weight_list(w):
    return [w['wka'], w['wkb'], w['wva'], w['wvb'], w['w2k'], w['w2v'],
            w['peka'], w['pekb'], w['peva'], w['pevb'], w['g_nsa_kc']]


def _compress_prompt(cmp_kv, w):
    b, s, _ = cmp_kv.shape
    rows = s // CMP_STRIDE
    rb = LANE
    x = cmp_kv.reshape(b, rows, CMP_STRIDE * 256)
    wl = _compress_weight_list(w)
    in_specs = [pl.BlockSpec((None, rb, CMP_STRIDE * 256), lambda bb, j: (bb, j, 0))] + [_full(a.shape) for a in wl]
    ospec = pl.BlockSpec((None, rb, LANE), lambda bb, j: (bb, j, 0))
    osh = jax.ShapeDtypeStruct((b, rows, LANE), BF16)
    return pl.pallas_call(
        functools.partial(_compress_body, 1), grid=(b, rows // rb), in_specs=in_specs,
        out_specs=[ospec, ospec], out_shape=[osh, osh],
        scratch_shapes=[pltpu.VMEM((1, 2 * CMP_HIDDEN), F32), pltpu.VMEM((1, 2 * CMP_HIDDEN), F32)],
        compiler_params=_params(("parallel", "arbitrary")), name="compress_prompt")(x, *wl)


def _compress_sample(cache, li, page_table, w, p):
    nseq, n_pages = page_table.shape
    rp = PAGE // CMP_STRIDE
    cache = cache.reshape(cache.shape[0], cache.shape[1], rp, CMP_STRIDE * 256)
    wl = _compress_weight_list(w)
    page_specs = [pl.BlockSpec((None, None, rp, CMP_STRIDE * 256),
                               functools.partial(lambda bb, j, pt, pp: (li, pt[bb, j * p + pp], 0, 0), pp=pp))
                  for pp in range(p)]
    in_specs = page_specs + [pl.BlockSpec(a.shape, functools.partial(lambda bb, j, pt, nd: (0,) * nd, nd=a.ndim))
                             for a in wl]
    ospec = pl.BlockSpec((None, rp * p, LANE), lambda bb, j, pt: (bb, j, 0))
    osh = jax.ShapeDtypeStruct((nseq, rp * n_pages, LANE), BF16)
    gs = pltpu.PrefetchScalarGridSpec(
        num_scalar_prefetch=1, grid=(nseq, n_pages // p), in_specs=in_specs, out_specs=[ospec, ospec],
        scratch_shapes=[pltpu.VMEM((1, 2 * CMP_HIDDEN), F32), pltpu.VMEM((1, 2 * CMP_HIDDEN), F32)])
    return pl.pallas_call(
        lambda pt, *r: _compress_body(p, *r), grid_spec=gs, out_shape=[osh, osh],
        compiler_params=_params(("parallel", "arbitrary")), name="compress_sample",
    )(page_table, *([cache] * p), *wl)


def _cmp_select_body(t, n_cmp, slopes, q_ref, kc_ref, vc_ref, ov_ref, gate_ref, o_ref, sb_ref):
    i = pl.program_id(1)
    ncp = kc_ref.shape[0]
    qpos = i * t + _iota((t, 1), 0)
    ridx = _iota((1, ncp), 1)
    cmp_end = ridx * CMP_STRIDE + (CMP_STRIDE - 1)
    valid = (ridx >= 1) & (ridx <= n_cmp) & (cmp_end <= qpos)
    dist = (qpos - cmp_end).astype(F32)
    kc, vc, ov = kc_ref[...], vc_ref[...], ov_ref[...]
    blk = _iota((t, LANE), 1)
    cur = qpos // SEL_BLOCK
    started = blk * SEL_BLOCK <= qpos
    forced = (blk == 0) | (blk == cur) | (blk == cur - 1)
    for kvh in range(N_KV):
        psum = jnp.zeros((t, ncp), F32)
        for g in range(GROUP):
            hh = kvh * GROUP + g
            s = _dot_nt(q_ref[:, hh * LANE:(hh + 1) * LANE], kc) - float(slopes[hh]) * dist
            s = jnp.where(valid, s, NEG)
            m = jnp.max(s, axis=-1, keepdims=True)
            p = jnp.where(valid, jnp.exp(s - m), 0.0)
            p = p / jnp.maximum(jnp.sum(p, axis=-1, keepdims=True), 1e-30)
            o = _dot(p.astype(BF16), vc)[:, kvh * HEAD_DIM:(kvh + 1) * HEAD_DIM]
            o_ref[:, hh * HEAD_DIM:(hh + 1) * HEAD_DIM] = o * gate_ref[:, 3 * hh:3 * hh + 1]
            psum = psum + p
        hi, lo = _split2(psum)
        imp = _dot(hi, ov) + _dot(lo, ov)
        sc = jnp.where(started, jnp.where(forced, -NEG, imp), NEG)
        sel = _topk_select(sc, SEL_TOPN)
        sb_ref[:, kvh * LANE:(kvh + 1) * LANE] = jnp.where(sel > 0.5, 0.0, BLOCK_NEG).astype(BF16)


def _cmp_select(nq, kc, vc, ov, gates, t, slopes):
    b, s, _ = nq.shape
    ncp = kc.shape[1]
    n_cmp = (s - CMP_LEN) // CMP_STRIDE + 1
    return pl.pallas_call(
        functools.partial(_cmp_select_body, t, n_cmp, slopes), grid=(b, s // t),
        in_specs=[pl.BlockSpec((None, t, 1024), lambda bb, i: (bb, i, 0)),
                  pl.BlockSpec((None, ncp, LANE), lambda bb, i: (bb, 0, 0)),
                  pl.BlockSpec((None, ncp, LANE), lambda bb, i: (bb, 0, 0)),
                  pl.BlockSpec(ov.shape, lambda bb, i: (0, 0)),
                  pl.BlockSpec((None, t, 3 * N_HEADS), lambda bb, i: (bb, i, 0))],
        out_specs=[pl.BlockSpec((None, t, 512), lambda bb, i: (bb, i, 0)),
                   pl.BlockSpec((None, t, 2 * LANE), lambda bb, i: (bb, i, 0))],
        out_shape=[jax.ShapeDtypeStruct((b, s, 512), F32), jax.ShapeDtypeStruct((b, s, 2 * LANE), BF16)],
        compiler_params=_params(("parallel", "parallel")), name="nsa_cmp_select")(nq, kc, vc, ov, gates)


def _moba_select_body(t, qf_ref, km_ref, sb_ref):
    i = pl.program_id(1)
    qpos = i * t + _iota((t, 1), 0)
    own = qpos // MOBA_BLOCK
    blk = _iota((t, LANE), 1)
    kh, kl = _split2(km_ref[...])
    for hh in range(N_HEADS):
        qh, ql = _split2(qf_ref[:, hh * LANE:(hh + 1) * LANE])
        sc = _dot_nt(qh, kh) + (_dot_nt(qh, kl) + _dot_nt(ql, kh))
        sel = _topk_select(jnp.where(blk < own, sc, NEG), MOBA_TOPK)
        keep = (sel > 0.5) | (blk == own)
        sb_ref[:, hh * LANE:(hh + 1) * LANE] = jnp.where(keep, 0.0, BLOCK_NEG).astype(BF16)


def _moba_select(mqf, kmean, t):
    b, s, _ = mqf.shape
    return pl.pallas_call(
        functools.partial(_moba_select_body, t), grid=(b, s // t),
        in_specs=[pl.BlockSpec((None, t, 1024), lambda bb, i: (bb, i, 0)),
                  pl.BlockSpec((None, LANE, LANE), lambda bb, i: (bb, 0, 0))],
        out_specs=pl.BlockSpec((None, t, 1024), lambda bb, i: (bb, i, 0)),
        out_shape=jax.ShapeDtypeStruct((b, s, 1024), BF16),
        compiler_params=_params(("parallel", "parallel")), name="moba_select")(mqf, kmean)


def _row_query(shape, n_heads=N_HEADS):
    return _iota(shape, 0) // n_heads


def _row_kv_is1(shape):
    return (_iota(shape, 0) % N_HEADS) >= GROUP


def _pick_half(acc):
    r = acc.shape[0]
    return jnp.where(_row_kv_is1((r, HEAD_DIM)), acc[:, HEAD_DIM:2 * HEAD_DIM], acc[:, 0:HEAD_DIM])


def _page_spec(li, width, p, pp, col=0):
    return pl.BlockSpec((None, None, PAGE, width),
                        lambda bb, j, pt: (li, pt[bb, j * p + pp], 0, col))


def _const_spec(a):
    nd = a.ndim
    return pl.BlockSpec(a.shape, lambda bb, j, pt: (0,) * nd)


def _seq_spec(shape_tail):
    nd = len(shape_tail)
    return pl.BlockSpec((None,) + tuple(shape_tail), lambda bb, j, pt: (bb,) + (0,) * nd)


def _sample_init(j, m_ref, l_ref, acc_ref):
    @pl.when(j == 0)
    def _():
        m_ref[...] = jnp.full(m_ref.shape, NEG, F32)
        l_ref[...] = jnp.zeros(l_ref.shape, F32)
        acc_ref[...] = jnp.zeros(acc_ref.shape, F32)


def _new_rows_mask(rows, n_new):
    col = _iota((rows, PAGE), 1)
    return (col <= _row_query((rows, PAGE))) & (col < n_new)


def _s_fox_body(p, n_new, pt_ref, q_ref, *refs):
    kv_refs = refs[:p]
    lf_refs = refs[p:2 * p]
    nkv_ref, nlf_ref, o_ref, m_ref, l_ref, acc_ref, carry_ref = refs[2 * p:]
    j = pl.program_id(1)
    _sample_init(j, m_ref, l_ref, acc_ref)

    @pl.when(j == 0)
    def _():
        carry_ref[...] = jnp.zeros_like(carry_ref)

    q = q_ref[...]
    carry = carry_ref[...]
    s_list, v_list = [], []
    for pp in range(p):
        cum = _cumsum_lanes(lf_refs[pp][...]) + carry
        carry = cum[:, PAGE - 1:PAGE]
        kv = kv_refs[pp][...]
        s_list.append(_dot_nt(q, kv[:, 0:LANE].astype(BF16)) - jnp.concatenate([cum] * 4, axis=0))
        v_list.append(kv[:, LANE:2 * LANE].astype(BF16))
    carry_ref[...] = carry
    _online_update(jnp.concatenate(s_list, axis=1), v_list, m_ref, l_ref, acc_ref)

    @pl.when(j == pl.num_programs(1) - 1)
    def _():
        cum = _cumsum_lanes(nlf_ref[...]) + carry
        kv = nkv_ref[...]
        s = _dot_nt(q, kv[:, 0:LANE].astype(BF16)) - jnp.concatenate([cum] * 4, axis=0)
        s = jnp.where(_new_rows_mask(s.shape[0], n_new), s, NEG)
        _online_update(s, [kv[:, LANE:2 * LANE].astype(BF16)], m_ref, l_ref, acc_ref)
        o_ref[...] = _pick_half(acc_ref[...] / l_ref[...])


def _s_fox(q, cache_kv, cache_lft, li, page_table, new_kv, new_lft, p, n_new):
    nseq, n_pages = page_table.shape
    rows = q.shape[1]
    in_specs = ([_seq_spec((rows, LANE))] + [_page_spec(li, 256, p, pp) for pp in range(p)]
                + [pl.BlockSpec((None, None, N_HEADS, PAGE),
                                functools.partial(lambda bb, j, pt, pp: (li, pt[bb, j * p + pp], 0, 0), pp=pp))
                   for pp in range(p)]
                + [_seq_spec((PAGE, 256)), _seq_spec((N_HEADS, PAGE))])
    gs = pltpu.PrefetchScalarGridSpec(
        num_scalar_prefetch=1, grid=(nseq, n_pages // p), in_specs=in_specs,
        out_specs=_seq_spec((rows, HEAD_DIM)),
        scratch_shapes=[pltpu.VMEM((rows, 1), F32), pltpu.VMEM((rows, 1), F32), pltpu.VMEM((rows, LANE), F32),
                        pltpu.VMEM((N_HEADS, 1), F32)])
    return pl.pallas_call(
        functools.partial(_s_fox_body, p, n_new), grid_spec=gs,
        out_shape=jax.ShapeDtypeStruct((nseq, rows, HEAD_DIM), F32),
        compiler_params=_params(("parallel", "arbitrary")), name="sample_fox",
    )(page_table, q, *([cache_kv] * p), *([cache_lft] * p), new_kv, new_lft)


def _s_mla_body(p, n_new, pt_ref, q_ref, *refs):
    lat_refs = refs[:p]
    (nlat_ref, wkslot_ref, gkn_ref, wkc_ref, ind_ref, wv_ref, o_ref,
     m_ref, l_ref, acc_ref, qa_ref, qr_ref) = refs[p:]
    j = pl.program_id(1)
    _sample_init(j, m_ref, l_ref, acc_ref)
    rows = q_ref.shape[0]
    head = _iota((rows, LANE), 0) % N_HEADS

    @pl.when(j == 0)
    def _():
        q = q_ref[...].astype(F32)
        qa = jnp.zeros((rows, LANE), F32)
        qr = jnp.zeros((rows, LANE), F32)
        for hh in range(N_HEADS):
            slot = q[:, hh * LANE:(hh + 1) * LANE]
            qn = (slot * gkn_ref[...]).astype(BF16)
            qa = jnp.where(head == hh, _dot_nt(qn, wkslot_ref[:, hh * LANE:(hh + 1) * LANE]), qa)
            qr = jnp.where(head == hh, pltpu.roll(slot, 64, 1), qr)
        qa_ref[...] = qa.astype(BF16)
        lane = _iota((rows, LANE), 1)
        qr_ref[...] = jnp.where(lane < MLA_ROPE, qr, 0.0).astype(BF16)

    qa, qr = qa_ref[...], qr_ref[...][:, 0:MLA_ROPE]
    ind = ind_ref[...]
    wkc = wkc_ref[...]

    def logits(lat):
        latc = lat[:, 0:MLA_KV_LORA].astype(BF16)
        latr = lat[:, MLA_KV_LORA:MLA_KV_LORA + MLA_ROPE].astype(BF16)
        kraw = _dot(latc, wkc)
        hi, lo = _split2(kraw * kraw)
        ms = _dot_nt(ind, hi) + _dot_nt(ind, lo)
        r = lax.rsqrt(ms + NORM_EPS)
        return _dot_nt(qa, latc) * jnp.concatenate([r] * (rows // N_HEADS), axis=0) + _dot_nt(qr, latr), latc

    s_list, v_list = [], []
    for pp in range(p):
        s, latc = logits(lat_refs[pp][...])
        s_list.append(s)
        v_list.append(latc)
    _online_update(jnp.concatenate(s_list, axis=1), v_list, m_ref, l_ref, acc_ref)

    @pl.when(j == pl.num_programs(1) - 1)
    def _():
        s, latc = logits(nlat_ref[...])
        s = jnp.where(_new_rows_mask(rows, n_new), s, NEG)
        _online_update(s, [latc], m_ref, l_ref, acc_ref)
        olat = (acc_ref[...] / l_ref[...]).astype(BF16)
        hd = _iota((rows, MLA_V), 0) % N_HEADS
        o = jnp.zeros((rows, MLA_V), F32)
        for hh in range(N_HEADS):
            o = jnp.where(hd == hh, _dot(olat, wv_ref[:, hh * MLA_V:(hh + 1) * MLA_V]), o)
        o_ref[...] = o


def _s_mla(q, cache, li, page_table, new_lat, w, p, n_new):
    nseq, n_pages = page_table.shape
    rows = q.shape[1]
    consts = [w['w_k_slot_t'], w['g_k_nope'], w['w_k_c'], w['ind_k'], w['w_v']]
    in_specs = ([_seq_spec((rows, N_HEADS * LANE))] + [_page_spec(li, 160, p, pp) for pp in range(p)]
                + [_seq_spec((PAGE, 160))] + [_const_spec(a) for a in consts])
    gs = pltpu.PrefetchScalarGridSpec(
        num_scalar_prefetch=1, grid=(nseq, n_pages // p), in_specs=in_specs,
        out_specs=_seq_spec((rows, MLA_V)),
        scratch_shapes=[pltpu.VMEM((rows, 1), F32), pltpu.VMEM((rows, 1), F32), pltpu.VMEM((rows, LANE), F32),
                        pltpu.VMEM((rows, LANE), BF16), pltpu.VMEM((rows, LANE), BF16)])
    return pl.pallas_call(
        functools.partial(_s_mla_body, p, n_new), grid_spec=gs,
        out_shape=jax.ShapeDtypeStruct((nseq, rows, MLA_V), F32),
        compiler_params=_params(("parallel", "arbitrary")), name="sample_mla",
    )(page_table, q, *([cache] * p), new_lat, *consts)


def _s_block_body(p, n_new, past_len, gate_idx, pt_ref, q_ref, sb_ref, sbn_ref, e_ref, slope_ref, *refs):
    kv_refs = refs[:p]
    rest = list(refs[p:])
    nkv_ref = rest.pop(0)
    gate_ref = rest.pop(0) if gate_idx is not None else None
    o_ref, m_ref, l_ref, acc_ref = rest
    j = pl.program_id(1)
    _sample_init(j, m_ref, l_ref, acc_ref)
    q = q_ref[...]
    rows = q.shape[0]
    slope = slope_ref[...]
    qpos = past_len + _row_query((rows, 1))
    kpos = j * (p * PAGE) + _iota((1, p * PAGE), 1)
    bias = _dot(sb_ref[...], e_ref[...]) - slope * (qpos - kpos).astype(F32)
    s_list, v_list = [], []
    for pp in range(p):
        kv = kv_refs[pp][...]
        s_list.append(_dot_nt(q, kv[:, 0:LANE].astype(BF16)))
        v_list.append(kv[:, LANE:2 * LANE].astype(BF16))
    _online_update(jnp.concatenate(s_list, axis=1) + bias, v_list, m_ref, l_ref, acc_ref)

    @pl.when(j == pl.num_programs(1) - 1)
    def _():
        kv = nkv_ref[...]
        npos = past_len + _iota((1, PAGE), 1)
        s = (_dot_nt(q, kv[:, 0:LANE].astype(BF16)) + sbn_ref[...][:, 0:1].astype(F32)
             - slope * (qpos - npos).astype(F32))
        s = jnp.where(_new_rows_mask(rows, n_new), s, NEG)
        _online_update(s, [kv[:, LANE:2 * LANE].astype(BF16)], m_ref, l_ref, acc_ref)
        o = _pick_half(acc_ref[...] / l_ref[...])
        if gate_ref is not None:
            o = o * gate_ref[:, gate_idx:gate_idx + 1]
        o_ref[...] = o


def _s_block(q, sb_chunks, sb_new, e_mat, slope_rows, cache, li, page_table, new_kv, p, n_new, gates=None,
             gate_idx=None):
    nseq, n_pages = page_table.shape
    rows = q.shape[1]
    past_len = n_pages * PAGE
    in_specs = ([_seq_spec((rows, LANE)),
                 pl.BlockSpec((None, None, rows, LANE), lambda bb, j, pt: (bb, j, 0, 0)),
                 _seq_spec((rows, LANE)), _const_spec(e_mat), _const_spec(slope_rows)]
                + [_page_spec(li, 256, p, pp) for pp in range(p)] + [_seq_spec((PAGE, 256))])
    ins = [q, sb_chunks, sb_new, e_mat, slope_rows] + [cache] * p + [new_kv]
    if gates is not None:
        in_specs.append(_seq_spec((rows, 3)))
        ins.append(gates)
    gs = pltpu.PrefetchScalarGridSpec(
        num_scalar_prefetch=1, grid=(nseq, n_pages // p), in_specs=in_specs,
        out_specs=_seq_spec((rows, HEAD_DIM)),
        scratch_shapes=[pltpu.VMEM((rows, 1), F32), pltpu.VMEM((rows, 1), F32), pltpu.VMEM((rows, LANE), F32)])
    return pl.pallas_call(
        functools.partial(_s_block_body, p, n_new, past_len, gate_idx), grid_spec=gs,
        out_shape=jax.ShapeDtypeStruct((nseq, rows, HEAD_DIM), F32),
        compiler_params=_params(("parallel", "arbitrary")), name="sample_block_attn",
    )(page_table, *ins)


def _s_cmp_body(n_new, past_len, q_ref, kc_ref, vc_ref, ov_ref, grp_ref, grpt_ref, slope_ref, gate_ref,
                o_ref, sb_ref):
    q = q_ref[...]
    rows = q.shape[0]
    ncp = kc_ref.shape[0]
    qpos = past_len + _row_query((rows, 1))
    ridx = _iota((1, ncp), 1)
    cmp_end = ridx * CMP_STRIDE + (CMP_STRIDE - 1)
    valid = (ridx >= 1) & (cmp_end <= qpos)
    s = _dot_nt(q, kc_ref[...]) - slope_ref[...] * (qpos - cmp_end).astype(F32)
    s = jnp.where(valid, s, NEG)
    m = jnp.max(s, axis=-1, keepdims=True)
    pr = jnp.where(valid, jnp.exp(s - m), 0.0)
    pr = pr / jnp.maximum(jnp.sum(pr, axis=-1, keepdims=True), 1e-30)
    o_ref[...] = _pick_half(_dot(pr.astype(BF16), vc_ref[...])) * gate_ref[:, 0:1]
    hi, lo = _split2(pr)
    grp = grp_ref[...]
    psum = _dot(grp, hi) + _dot(grp, lo)
    hi, lo = _split2(psum)
    imp = _dot(hi, ov_ref[...]) + _dot(lo, ov_ref[...])
    g_rows = imp.shape[0]
    gq = past_len + _iota(imp.shape, 0) // N_KV
    blk = _iota(imp.shape, 1)
    cur = gq // SEL_BLOCK
    started = blk * SEL_BLOCK <= gq
    forced = (blk == 0) | (blk == cur) | (blk == cur - 1)
    sel = _topk_select(jnp.where(started, jnp.where(forced, -NEG, imp), NEG), SEL_TOPN)
    sbg = jnp.where(sel > 0.5, 0.0, BLOCK_NEG).astype(BF16)
    sb_ref[...] = _dot(grpt_ref[...], sbg).astype(BF16)


def _s_cmp(q, kc, vc, ov, grp, grpt, slope_rows, gates, n_new, past_len):
    nseq, rows, _ = q.shape
    ncp = kc.shape[1]
    nsbp = ov.shape[1]
    seq = lambda tail: pl.BlockSpec((None,) + tuple(tail), lambda bb: (bb,) + (0,) * len(tail))
    cst = lambda a: pl.BlockSpec(a.shape, lambda bb: (0,) * a.ndim)
    return pl.pallas_call(
        functools.partial(_s_cmp_body, n_new, past_len), grid=(nseq,),
        in_specs=[seq((rows, LANE)), seq((ncp, LANE)), seq((ncp, LANE)), cst(ov), cst(grp), cst(grpt),
                  cst(slope_rows), seq((rows, 3))],
        out_specs=[seq((rows, HEAD_DIM)), seq((rows, nsbp))],
        out_shape=[jax.ShapeDtypeStruct((nseq, rows, HEAD_DIM), F32),
                   jax.ShapeDtypeStruct((nseq, rows, nsbp), BF16)],
        compiler_params=_params(("parallel",)), name="sample_cmp_select")(q, kc, vc, ov, grp, grpt, slope_rows, gates)


def _s_moba_score_body(p, own_blk, pt_ref, qf_ref, *refs):
    k_refs = refs[:p]
    sb_ref, sc_ref = refs[p:]
    j = pl.program_id(1)
    rows = qf_ref.shape[0]
    lane = _iota((rows, LANE), 1)

    @pl.when(j == 0)
    def _():
        sc_ref[...] = jnp.full(sc_ref.shape, NEG, F32)

    qf = qf_ref[...]
    ppb = MOBA_BLOCK // PAGE
    sc = sc_ref[...]
    for b in range(p // ppb):
        ks = jnp.sum(k_refs[b * ppb][...], axis=0, keepdims=True)
        for e in range(1, ppb):
            ks = ks + jnp.sum(k_refs[b * ppb + e][...], axis=0, keepdims=True)
        val = jnp.sum(qf * (ks * (1.0 / MOBA_BLOCK)), axis=-1, keepdims=True)
        sc = jnp.where(lane == j * (p // ppb) + b, val, sc)
    sc_ref[...] = sc

    @pl.when(j == pl.num_programs(1) - 1)
    def _():
        sel = _topk_select(jnp.where(lane < own_blk, sc, NEG), MOBA_TOPK)
        keep = (sel > 0.5) | (lane == own_blk)
        sb_ref[...] = jnp.where(keep, 0.0, BLOCK_NEG).astype(BF16)


def _s_moba_score(qf, cache, li, page_table, p):
    nseq, n_pages = page_table.shape
    rows = qf.shape[1]
    own_blk = (n_pages * PAGE) // MOBA_BLOCK
    in_specs = [_seq_spec((rows, LANE))] + [_page_spec(li, LANE, p, pp) for pp in range(p)]
    gs = pltpu.PrefetchScalarGridSpec(
        num_scalar_prefetch=1, grid=(nseq, n_pages // p), in_specs=in_specs,
        out_specs=_seq_spec((rows, LANE)), scratch_shapes=[pltpu.VMEM((rows, LANE), F32)])
    return pl.pallas_call(
        functools.partial(_s_moba_score_body, p, own_blk), grid_spec=gs,
        out_shape=jax.ShapeDtypeStruct((nseq, rows, LANE), BF16),
        compiler_params=_params(("parallel", "arbitrary")), name="sample_moba_score",
    )(page_table, qf, *([cache] * p))


def _s_win_body(n_new, past_len, q_ref, st_ref, nkv_ref, slope_ref, gate_ref, o_ref):
    q = q_ref[...]
    rows = q.shape[0]
    wb = st_ref.shape[0]
    slope = slope_ref[...]
    qpos = past_len + _row_query((rows, 1))
    st = st_ref[...]
    wpos = past_len - wb + _iota((1, wb), 1)
    d1 = qpos - wpos
    s1 = _dot_nt(q, st[:, 0:LANE].astype(BF16)) - slope * d1.astype(F32)
    s1 = jnp.where((d1 >= 0) & (d1 < WINDOW) & (wpos >= 0), s1, NEG)
    nkv = nkv_ref[...]
    d2 = qpos - (past_len + _iota((1, PAGE), 1))
    s2 = _dot_nt(q, nkv[:, 0:LANE].astype(BF16)) - slope * d2.astype(F32)
    s2 = jnp.where(_new_rows_mask(rows, n_new), s2, NEG)
    m = jnp.maximum(jnp.max(s1, axis=-1, keepdims=True), jnp.max(s2, axis=-1, keepdims=True))
    p1, p2 = jnp.exp(s1 - m), jnp.exp(s2 - m)
    den = jnp.sum(p1, axis=-1, keepdims=True) + jnp.sum(p2, axis=-1, keepdims=True)
    acc = _dot(p1.astype(BF16), st[:, LANE:2 * LANE].astype(BF16)) \
        + _dot(p2.astype(BF16), nkv[:, LANE:2 * LANE].astype(BF16))
    o_ref[...] = _pick_half(acc / den) * gate_ref[:, 2:3]


def _s_win(q, state, li, new_kv, slope_rows, gates, n_new, past_len):
    nseq, rows, _ = q.shape
    wb = state.shape[2]
    seq = lambda tail: pl.BlockSpec((None,) + tuple(tail), lambda bb: (bb,) + (0,) * len(tail))
    return pl.pallas_call(
        functools.partial(_s_win_body, n_new, past_len), grid=(nseq,),
        in_specs=[seq((rows, LANE)), pl.BlockSpec((None, None, wb, 256), lambda bb: (li, bb, 0, 0)),
                  seq((PAGE, 256)), pl.BlockSpec(slope_rows.shape, lambda bb: (0, 0)), seq((rows, 3))],
        out_specs=seq((rows, HEAD_DIM)),
        out_shape=jax.ShapeDtypeStruct((nseq, rows, HEAD_DIM), F32),
        compiler_params=_params(("parallel",)), name="sample_window")(q, state, new_kv, slope_rows, gates)


def _onehot_blocks(n_keys, block, width=LANE):
    e = (np.arange(n_keys)[:, None] // block) == np.arange(width)[None, :]
    return jnp.asarray(e, BF16)


def _overlap_rows(n_rows, n_cols):
    i0 = (np.arange(n_rows)[:, None] - 1) * CMP_STRIDE
    j0 = np.arange(n_cols)[None, :] * SEL_BLOCK
    ov = (i0 < j0 + SEL_BLOCK) & (i0 + CMP_LEN > j0) & (np.arange(n_rows)[:, None] >= 1)
    return jnp.asarray(ov, BF16)


def _tiles(n_rows, prefer):
    for t in (prefer, 512, 256, 128, 64, 32, 16, 8):
        if t <= n_rows and n_rows % t == 0:
            return t
    return n_rows


def _sample_rows(a, nseq, n_tok):
    return a.reshape(nseq, n_tok * N_HEADS, LANE)


def _pad_new(a, nseq, n_tok):
    a = a.reshape(nseq, n_tok, a.shape[-1])
    return jnp.pad(a, ((0, 0), (0, PAGE - n_tok), (0, 0)))


def _rows_out(o, nseq, n_tok):
    return o.reshape(nseq * n_tok, N_HEADS * HEAD_DIM)


def kernel(x_prompt, x_sample, cache_mla, cache_fox_kv, cache_fox_logf, cache_nsa_cmp_kv, cache_nsa_sel_kv,
           state_nsa_win_kv, cache_moba_kv, page_table, g_mix, g_ffn, w_ffn_up, w_ffn_down, w_in_ab, w_out_ab,
           g_mla_q_lora, w_mla_q_up, g_mla_q_nope, g_mla_q_rope, g_mla_kv_lora, g_mla_k_rope, w_mla_kv_up,
           g_mla_k_nope, g_fox_q, g_fox_k, b_fox_f, w_in_cd, w_out_cd, g_nsa_q, g_nsa_kc, g_nsa_ks, g_nsa_kw,
           nsa_pe_k, nsa_pe_v, w_nsa_ck1, w_nsa_ck2, w_nsa_cv1, w_nsa_cv2, b_nsa_gate, g_moba_q, g_moba_k):
    nb, seq, d = x_prompt.shape
    nseq, n_tok, _ = x_sample.shape
    n_pages = page_table.shape[1]
    past_len = n_pages * PAGE
    depth = g_mix.shape[0]
    assert seq % 2048 == 0 and seq // SEL_BLOCK <= LANE and n_pages % 4 == 0 and n_tok * N_HEADS <= 32
    rows = n_tok * N_HEADS
    t_attn = 512
    t_sel = 256
    tm_p = _tiles(nb * seq, 512)
    tm_s = _tiles(nseq * n_tok, 512)
    p_pages = 16 if n_pages % 16 == 0 and n_pages >= 32 else 4
    slopes_n, slopes_m = _alibi_slopes()

    xp = x_prompt.reshape(nb * seq, d)
    xs = x_sample.reshape(nseq * n_tok, d)
    pos_p = _rope_tables(jnp.arange(seq))
    pos_s_tab = _rope_tables(past_len + jnp.arange(n_tok))
    reps = tm_s // n_tok
    pos_s = (jnp.tile(pos_s_tab[0], (reps, 1)), jnp.tile(pos_s_tab[1], (reps, 1)))

    qrow = np.arange(rows) // N_HEADS
    hrow = np.arange(rows) % N_HEADS
    slope_rows_n = jnp.asarray(slopes_n[hrow].reshape(rows, 1), F32)
    slope_rows_m = jnp.asarray(slopes_m[hrow].reshape(rows, 1), F32)
    n_grp = n_tok * N_KV
    grp_np = ((qrow * N_KV + hrow // GROUP)[None, :] == np.arange(n_grp)[:, None])
    grp = jnp.asarray(grp_np, BF16)
    grpt = jnp.asarray(grp_np.T, BF16)
    e_sel_s = _onehot_blocks(p_pages * PAGE, SEL_BLOCK).T
    e_mb_s = _onehot_blocks(p_pages * PAGE, MOBA_BLOCK).T
    nsb_s = -(-(past_len + n_tok) // SEL_BLOCK)
    nsbp_s = -(-nsb_s // LANE) * LANE
    ov_s = _overlap_rows(past_len // CMP_STRIDE, nsbp_s)
    e_sel_p = _onehot_blocks(seq, SEL_BLOCK)
    e_mb_p = _onehot_blocks(seq, MOBA_BLOCK)
    ov_p = _overlap_rows(seq // CMP_STRIDE, LANE)

    lists = [[] for _ in range(14)]
    (mla_p, mla_s, fkv_p, fkv_s, flf_p, flf_s, cmp_p, cmp_s, sel_p, sel_s, win_p, win_s, mb_p, mb_s) = lists

    for layer in range(depth):
        li = layer // 2
        gm = g_mix[layer].reshape(1, d)
        if layer % 2 == 0:
            w = dict(
                g_mix=gm, w_in=_take_cols(w_in_ab[li], _ab_col_index()).astype(BF16),
                g_q_lora=g_mla_q_lora[li].reshape(1, -1),
                w_q_up=_take_cols(w_mla_q_up[li], _slots_idx(0, N_HEADS, MLA_NOPE + MLA_ROPE, lambda h: 0)).astype(BF16),
                g_q=jnp.concatenate([_pad_lanes(g_mla_q_nope[li], 0, 64), _pad_lanes(g_mla_q_rope[li], 0, 64)], axis=1),
                g_kpe=_pad_lanes(g_mla_k_rope[li], 64),
                g_kv_lora=g_mla_kv_lora[li].reshape(1, -1),
                g_k_nope=_pad_lanes(g_mla_k_nope[li], 0),
                g_fox_q=jnp.tile(g_fox_q[li].reshape(1, -1), (1, 2)),
                g_fox_k=jnp.tile(g_fox_k[li].reshape(1, -1), (1, 2)),
                b_fox_f=_pad_lanes(b_fox_f[li], 0))
            wkv = w_mla_kv_up[li].reshape(MLA_KV_LORA, N_HEADS, MLA_NOPE + MLA_V)
            wk_c = wkv[:, :, :MLA_NOPE].reshape(MLA_KV_LORA, N_HEADS * MLA_NOPE)
            w['w_k'] = _take_cols(wk_c, _slots_idx(0, N_HEADS, MLA_NOPE, lambda h: 0)).astype(BF16)
            w['w_v'] = wkv[:, :, MLA_NOPE:].reshape(MLA_KV_LORA, N_HEADS * MLA_V).astype(BF16)
            w['w_k_c'] = wk_c.astype(BF16)
            w['w_k_slot_t'] = w['w_k']
            ind = (np.arange(N_HEADS * MLA_NOPE)[None, :] // MLA_NOPE) == np.arange(N_HEADS)[:, None]
            w['ind_k'] = jnp.asarray(ind / MLA_NOPE, BF16)
            w_out = w_out_ab[li].astype(BF16)

            q, lat, mk, mv, fq, fkv, fkvb, lf = _proj_ab(xp, pos_p, w, tm_p)
            b3 = lambda a: a.reshape(nb, seq, a.shape[-1])
            cumt = _cumsum_seq(jnp.swapaxes(b3(lf), 1, 2))
            cum = jnp.swapaxes(cumt, 1, 2)
            om = _flash('mla', b3(q), b3(mk), b3(mv), t_attn)
            of = _flash('fox', b3(fq), b3(fkvb), b3(fkvb), t_attn, extra=(cum, cumt))
            xp = _post(xp, [om.reshape(nb * seq, 512)], of.reshape(nb * seq, 512), w_out,
                       g_ffn[layer].reshape(1, d), w_ffn_up[layer].astype(BF16), w_ffn_down[layer].astype(BF16), tm_p)
            mla_p.append(lat.reshape(nb, seq, 160))
            fkv_p.append(fkv.reshape(nb, seq, 2, N_KV, HEAD_DIM))
            flf_p.append(lf.reshape(nb, seq, N_HEADS))

            q, lat, _, _, fq, fkv, _, lf = _proj_ab(xs, pos_s, w, tm_s)
            om = _s_mla(_mla_rows(q, nseq, n_tok), cache_mla, li, page_table, _pad_new(lat, nseq, n_tok), w,
                        p_pages, n_tok)
            lft_cache = jnp.swapaxes(cache_fox_logf, 2, 3)
            new_lft = jnp.swapaxes(_pad_new(lf, nseq, n_tok), 1, 2)
            of = _s_fox(_sample_rows(fq, nseq, n_tok),
                        cache_fox_kv.reshape(cache_fox_kv.shape[0], cache_fox_kv.shape[1], PAGE, 256),
                        lft_cache, li, page_table, _pad_new(fkv, nseq, n_tok), new_lft, p_pages, n_tok)
            xs = _post(xs, [_rows_out(om, nseq, n_tok)], _rows_out(of, nseq, n_tok), w_out,
                       g_ffn[layer].reshape(1, d), w_ffn_up[layer].astype(BF16), w_ffn_down[layer].astype(BF16), tm_s)
            mla_s.append(lat.reshape(nseq, n_tok, 160))
            fkv_s.append(fkv.reshape(nseq, n_tok, 2, N_KV, HEAD_DIM))
            flf_s.append(lf.reshape(nseq, n_tok, N_HEADS))
        else:
            pair = lambda g: jnp.tile(g.reshape(1, -1), (1, 2))
            w = dict(g_mix=gm, w_in=_take_cols(w_in_cd[li], _cd_col_index()).astype(BF16),
                     g_nsa_q=pair(g_nsa_q[li]), g_nsa_ks=pair(g_nsa_ks[li]), g_nsa_kw=pair(g_nsa_kw[li]),
                     b_gate=_pad_lanes(b_nsa_gate[li], 0), g_moba_q=pair(g_moba_q[li]), g_moba_k=pair(g_moba_k[li]),
                     g_nsa_kc=pair(g_nsa_kc[li]))
            w['wka'], w['wkb'], w['w2k'] = _compress_weights(w_nsa_ck1[li], w_nsa_ck2[li])
            w['wva'], w['wvb'], w['w2v'] = _compress_weights(w_nsa_cv1[li], w_nsa_cv2[li])
            w['peka'], w['pekb'] = _compress_pe(nsa_pe_k[li])
            w['peva'], w['pevb'] = _compress_pe(nsa_pe_v[li])
            w_out = w_out_cd[li].astype(BF16)

            (nq, cmpkv, selkv, selb16, winkv, winb16, gates, mq, mqf, mkv, mkvb, kmean) = _proj_cd(xp, w, tm_p, True)
            b3 = lambda a: a.reshape(nb, seq, a.shape[-1])
            kc, vc = _compress_prompt(b3(cmpkv), w)
            o_cmp, sb = _cmp_select(b3(nq), kc, vc, ov_p, b3(gates), t_sel, slopes_n)
            o_sel = _flash('sel', b3(nq), b3(selb16), b3(selb16), t_attn, extra=(e_sel_p, sb), gates=b3(gates),
                           gate_idx=1, slopes=slopes_n)
            o_win = _flash('win', b3(nq), b3(winb16), b3(winb16), t_attn, gates=b3(gates), gate_idx=2,
                           slopes=slopes_n)
            nblk = seq // MOBA_BLOCK
            km = jnp.pad(kmean.reshape(nb, nblk, LANE), ((0, 0), (0, LANE - nblk), (0, 0)))
            sbm = _moba_select(b3(mqf), km, t_sel)
            o_mb = _flash('moba', b3(mq), b3(mkvb), b3(mkvb), t_attn, extra=(e_mb_p, sbm), slopes=slopes_m)
            f2 = lambda a: a.reshape(nb * seq, 512)
            xp = _post(xp, [f2(o_cmp), f2(o_sel), f2(o_win)], f2(o_mb), w_out,
                       g_ffn[layer].reshape(1, d), w_ffn_up[layer].astype(BF16), w_ffn_down[layer].astype(BF16), tm_p)
            kv5 = lambda a, n1, n2: a.reshape(n1, n2, 2, N_KV, HEAD_DIM)
            cmp_p.append(kv5(cmpkv, nb, seq))
            sel_p.append(kv5(selkv, nb, seq))
            win_p.append(kv5(winkv, nb, seq)[:, -min(WINDOW, seq):])
            mb_p.append(kv5(mkv, nb, seq))

            (nq, cmpkv, selkv, _, winkv, _, gates, mq, mqf, mkv, _) = _proj_cd(xs, w, tm_s, False)
            c4 = lambda c: c.reshape(c.shape[0], c.shape[1], PAGE, 256)
            g_rows = gates.reshape(nseq, rows, 3)
            q_n = _sample_rows(nq, nseq, n_tok)
            kc, vc = _compress_sample(c4(cache_nsa_cmp_kv), li, page_table, w, p_pages)
            o_cmp, sb = _s_cmp(q_n, kc, vc, ov_s, grp, grpt, slope_rows_n, g_rows, n_tok, past_len)
            n_ch = n_pages // p_pages
            bpc = p_pages * PAGE // SEL_BLOCK
            sbc = sb[:, :, :n_ch * bpc].reshape(nseq, rows, n_ch, bpc).transpose(0, 2, 1, 3)
            sbc = jnp.pad(sbc, ((0, 0), (0, 0), (0, 0), (0, LANE - bpc)))
            sbn = jnp.pad(sb[:, :, n_ch * bpc:n_ch * bpc + 1], ((0, 0), (0, 0), (0, LANE - 1)))
            o_sel = _s_block(q_n, sbc, sbn, e_sel_s, slope_rows_n, c4(cache_nsa_sel_kv), li, page_table,
                             _pad_new(selkv, nseq, n_tok), p_pages, n_tok, gates=g_rows, gate_idx=1)
            win_all = jnp.concatenate([state_nsa_win_kv[li], kv5(winkv, nseq, n_tok)], axis=1)
            wb = state_nsa_win_kv.shape[2]
            st4 = state_nsa_win_kv.reshape(state_nsa_win_kv.shape[0], nseq, wb, 256)
            o_win = _s_win(q_n, st4, li, _pad_new(winkv, nseq, n_tok), slope_rows_n, g_rows, n_tok, past_len)
            sbm = _s_moba_score(_sample_rows(mqf, nseq, n_tok), c4(cache_moba_kv), li, page_table, p_pages)
            bpc_m = p_pages * PAGE // MOBA_BLOCK
            sbmc = sbm[:, :, :n_ch * bpc_m].reshape(nseq, rows, n_ch, bpc_m).transpose(0, 2, 1, 3)
            sbmc = jnp.pad(sbmc, ((0, 0), (0, 0), (0, 0), (0, LANE - bpc_m)))
            sbmn = jnp.pad(sbm[:, :, n_ch * bpc_m:n_ch * bpc_m + 1], ((0, 0), (0, 0), (0, LANE - 1)))
            o_mb = _s_block(_sample_rows(mq, nseq, n_tok), sbmc, sbmn, e_mb_s, slope_rows_m, c4(cache_moba_kv), li,
                            page_table, _pad_new(mkv, nseq, n_tok), p_pages, n_tok)
            r2 = lambda o: _rows_out(o, nseq, n_tok)
            xs = _post(xs, [r2(o_cmp), r2(o_sel), r2(o_win)], r2(o_mb), w_out,
                       g_ffn[layer].reshape(1, d), w_ffn_up[layer].astype(BF16), w_ffn_down[layer].astype(BF16), tm_s)
            cmp_s.append(kv5(cmpkv, nseq, n_tok))
            sel_s.append(kv5(selkv, nseq, n_tok))
            win_s.append(win_all[:, -wb:])
            mb_s.append(kv5(mkv, nseq, n_tok))

    return (xp.reshape(nb, seq, d), xs.reshape(nseq, n_tok, d),
            jnp.stack(mla_p), jnp.stack(mla_s), jnp.stack(fkv_p), jnp.stack(fkv_s), jnp.stack(flf_p),
            jnp.stack(flf_s), jnp.stack(cmp_p), jnp.stack(cmp_s), jnp.stack(sel_p), jnp.stack(sel_s),
            jnp.stack(win_p), jnp.stack(win_s), jnp.stack(mb_p), jnp.stack(mb_s))


def _mla_rows(q, nseq, n_tok):
    q = q.reshape(nseq, n_tok, 1, N_HEADS * LANE)
    return jnp.broadcast_to(q, (nseq, n_tok, N_HEADS, N_HEADS * LANE)).reshape(nseq, n_tok * N_HEADS, N_HEADS * LANE)
```

```python
import functools
import math

import numpy as np
import jax
import jax.numpy as jnp
from jax import lax
from jax.experimental import pallas as pl
from jax.experimental.pallas import tpu as pltpu

F32 = jnp.float32
BF16 = jnp.bfloat16

HEAD_DIM = 64
N_HEADS = 8
N_KV = 2
GROUP = N_HEADS // N_KV
MLA_Q_LORA = 256
MLA_KV_LORA = 128
MLA_NOPE = 64
MLA_ROPE = 32
MLA_V = 64
ROPE_THETA = 10000.0
CMP_LEN = 32
CMP_STRIDE = 16
CMP_HIDDEN = 128
SEL_BLOCK = 64
SEL_TOPN = 16
WINDOW = 512
MOBA_BLOCK = 256
MOBA_TOPK = 3
NORM_EPS = 1e-6
PAGE = 128
N_CHAINS = 1

LANE = 128
NEG = -1e30
BLOCK_NEG = -(2.0 ** 100)
VMEM_LIMIT = 56 * 1024 * 1024

MLA_SCALE = 1.0 / math.sqrt(MLA_NOPE + MLA_ROPE)
HD_SCALE = 1.0 / math.sqrt(HEAD_DIM)


def _dot(a, b):
    return jnp.dot(a, b, preferred_element_type=F32)


def _dot_nt(a, b):
    return lax.dot_general(a, b, (((1,), (1,)), ((), ())), preferred_element_type=F32)


def _split2(x):
    hi = x.astype(BF16)
    lo = (x - hi.astype(F32)).astype(BF16)
    return hi, lo


def _iota(shape, axis):
    return lax.broadcasted_iota(jnp.int32, shape, axis)


def _rms(x, g):
    ms = jnp.mean(x * x, axis=-1, keepdims=True)
    return x * lax.rsqrt(ms + NORM_EPS) * g


def _seg_norm(z, g, split, n_a, n_b=0):
    sq = z * z
    if split >= LANE:
        sa = jnp.sum(sq, axis=-1, keepdims=True)
        inv = lax.rsqrt(sa * (1.0 / n_a) + NORM_EPS)
    else:
        in_a = _iota(z.shape, 1) < split
        sa = jnp.sum(jnp.where(in_a, sq, 0.0), axis=-1, keepdims=True)
        sb = jnp.sum(jnp.where(in_a, 0.0, sq), axis=-1, keepdims=True)
        inv = jnp.where(in_a, lax.rsqrt(sa * (1.0 / n_a) + NORM_EPS),
                        lax.rsqrt(sb * (1.0 / n_b) + NORM_EPS))
    return z * inv * g


def _rope_slot(x, cos, sin):
    lane = _iota(x.shape, 1)
    rot = jnp.where(lane < 80, pltpu.roll(x, LANE - 16, 1), pltpu.roll(x, 16, 1))
    return x * cos + rot * sin


def _cumsum_lanes(x):
    lane = _iota(x.shape, 1)
    sh = 1
    while sh < x.shape[1]:
        x = x + jnp.where(lane >= sh, pltpu.roll(x, sh, 1), 0.0)
        sh *= 2
    return x


def _topk_select(sc, k):
    lane = _iota(sc.shape, 1).astype(F32)
    sel = jnp.zeros(sc.shape, F32)
    for _ in range(k):
        mx = jnp.max(sc, axis=-1, keepdims=True)
        first = jnp.min(jnp.where(sc == mx, lane, 1e9), axis=-1, keepdims=True)
        pick = (lane == first) & (mx > 0.5 * NEG)
        sel = jnp.where(pick, 1.0, sel)
        sc = jnp.where(pick, 3.0 * NEG, sc)
    return sel


def _softmax_step(s, v, m_prev, l_prev, a_prev, vt=False):
    m_new = jnp.maximum(m_prev, jnp.max(s, axis=-1, keepdims=True))
    alpha = jnp.exp(m_prev - m_new)
    p = jnp.exp(s - m_new)
    l_new = alpha * l_prev + jnp.sum(p, axis=-1, keepdims=True)
    pb = p.astype(BF16)
    a_new = alpha * a_prev + (_dot_nt(pb, v) if vt else _dot(pb, v))
    return m_new, l_new, a_new


def _chains_update(s_list, v_list, m_ref, l_ref, acc_ref, vt=False):
    n = len(s_list)
    prev = [(m_ref[c], l_ref[c], acc_ref[c]) for c in range(n)]
    nxt = [_softmax_step(s_list[c], v_list[c], *prev[c], vt=vt) for c in range(n)]
    for c in range(n):
        m_ref[c], l_ref[c], acc_ref[c] = nxt[c]


def _chains_merge(m_ref, l_ref, acc_ref):
    n = m_ref.shape[0]
    m = m_ref[0]
    for c in range(1, n):
        m = jnp.maximum(m, m_ref[c])
    w = [jnp.exp(m_ref[c] - m) for c in range(n)]
    l = w[0] * l_ref[0]
    acc = w[0] * acc_ref[0]
    for c in range(1, n):
        l = l + w[c] * l_ref[c]
        acc = acc + w[c] * acc_ref[c]
    return acc / l


def _lane_groups(x, n):
    w = x.shape[1] // n
    return [x[:, c * w:(c + 1) * w] for c in range(n)]


def _params(sem, vmem=VMEM_LIMIT):
    return pltpu.CompilerParams(dimension_semantics=sem, vmem_limit_bytes=vmem)


def _full(shape):
    nd = len(shape)
    return pl.BlockSpec(shape, lambda *_: (0,) * nd)


def _alibi_slopes():
    n = 2 * N_HEADS
    s = np.exp2(-8.0 * (np.arange(n, dtype=np.float32) + 1.0) / n).astype(np.float32)
    return s[0::2], s[1::2]


def _take_cols(w, idx):
    idx = np.asarray(idx)
    wz = jnp.concatenate([w, jnp.zeros((w.shape[0], 1), w.dtype)], axis=1)
    return wz[:, np.where(idx < 0, w.shape[1], idx)]


def _slots_idx(base, n_heads, width, lane_off_fn):
    idx = -np.ones((n_heads * LANE,), np.int64)
    for h in range(n_heads):
        o = lane_off_fn(h)
        idx[h * LANE + o:h * LANE + o + width] = base + h * width + np.arange(width)
    return idx


def _ab_col_index():
    c_q = np.arange(0, 256)
    c_kv = np.arange(256, 384)
    kpe = -np.ones((LANE,), np.int64)
    kpe[64:96] = np.arange(384, 416)
    fq = _slots_idx(416, N_HEADS, HEAD_DIM, lambda h: HEAD_DIM * (h // GROUP))
    fkv = np.arange(928, 1184)
    ff = -np.ones((LANE,), np.int64)
    ff[0:8] = np.arange(1184, 1192)
    return np.concatenate([c_q, c_kv, kpe, fq, fkv, ff])


def _cd_col_index():
    nq = _slots_idx(0, N_HEADS, HEAD_DIM, lambda h: HEAD_DIM * (h // GROUP))
    kv3 = np.arange(512, 1280)
    gl = -np.ones((LANE,), np.int64)
    gl[0:24] = np.arange(1280, 1304)
    mq = _slots_idx(1304, N_HEADS, HEAD_DIM, lambda h: HEAD_DIM * (h // GROUP))
    mkv = np.arange(1816, 2072)
    return np.concatenate([nq, kv3, gl, mq, mkv])


def _pad_lanes(v, off, total=LANE):
    v = v.reshape(1, -1).astype(F32)
    return jnp.pad(v, ((0, 0), (off, total - off - v.shape[1])))


def _rope_tables(pos):
    half = MLA_ROPE // 2
    inv = jnp.power(ROPE_THETA, -jnp.arange(half, dtype=F32) / half)
    ang = pos.astype(F32)[:, None] * inv[None, :]
    cos, sin = jnp.cos(ang), jnp.sin(ang)
    n = pos.shape[0]
    ones = jnp.ones((n, 64), F32)
    ctab = jnp.concatenate([ones, cos, cos, jnp.ones((n, 32), F32)], axis=1)
    stab = jnp.concatenate([jnp.zeros((n, 64), F32), -sin, sin, jnp.zeros((n, 32), F32)], axis=1)
    return ctab, stab


def _compress_weights(w1, w2):
    w1r = w1.reshape(2, CMP_STRIDE, HEAD_DIM, CMP_HIDDEN)
    z = jnp.zeros_like(w1r)
    h0 = jnp.concatenate([w1r, z], axis=-1)
    h1 = jnp.concatenate([z, w1r], axis=-1)
    big = jnp.stack([h0, h1], axis=2)
    big = big.reshape(2, CMP_STRIDE * 2 * HEAD_DIM, 2 * CMP_HIDDEN).astype(BF16)
    z2 = jnp.zeros_like(w2)
    w2b = jnp.concatenate([jnp.concatenate([w2, z2], axis=1),
                           jnp.concatenate([z2, w2], axis=1)], axis=0).astype(BF16)
    return big[0], big[1], w2b


def _compress_pe(pe):
    t = jnp.tile(pe.reshape(2, CMP_STRIDE, 1, HEAD_DIM), (1, 1, 2, 1))
    t = t.reshape(2, 1, CMP_STRIDE * 2 * HEAD_DIM).astype(F32)
    return t[0], t[1]


def _proj_ab_body(x_ref, gmix_ref, win_ref, gql_ref, wqup_ref, gq_ref, gkpe_ref, gkvl_ref,
                  wk_ref, wv_ref, gkn_ref, gfq_ref, gfk_ref, bf_ref, cos_ref, sin_ref,
                  q_out, lat_out, mk_out, mv_out, fq_out, fkv_out, fkvb_out, lf_out):
    h = _rms(x_ref[...], gmix_ref[...]).astype(BF16)
    z = _dot(h, win_ref[...])
    cos, sin = cos_ref[...], sin_ref[...]
    cqn = _rms(z[:, 0:256], gql_ref[...]).astype(BF16)
    qz = _dot(cqn, wqup_ref[...])
    gq = gq_ref[...]
    for hh in range(N_HEADS):
        slot = _seg_norm(qz[:, hh * LANE:(hh + 1) * LANE], gq, 64, MLA_NOPE, MLA_ROPE)
        q_out[:, hh * LANE:(hh + 1) * LANE] = (_rope_slot(slot, cos, sin) * MLA_SCALE).astype(BF16)
    ckvn = _rms(z[:, 256:384], gkvl_ref[...])
    krot = _rope_slot(_seg_norm(z[:, 384:512], gkpe_ref[...], LANE, MLA_ROPE), cos, sin)
    lat_out[:, 0:MLA_KV_LORA] = ckvn
    lat_out[:, MLA_KV_LORA:MLA_KV_LORA + MLA_ROPE] = pltpu.roll(krot, 64, 1)[:, 0:MLA_ROPE]
    cb = ckvn.astype(BF16)
    kraw = _dot(cb, wk_ref[...])
    gkn = gkn_ref[...]
    for hh in range(N_HEADS):
        slot = _seg_norm(kraw[:, hh * LANE:(hh + 1) * LANE], gkn, LANE, MLA_NOPE) + krot
        mk_out[:, hh * LANE:(hh + 1) * LANE] = slot.astype(BF16)
    mv_out[...] = _dot(cb, wv_ref[...]).astype(BF16)
    gfq = gfq_ref[...]
    for hh in range(N_HEADS):
        slot = _seg_norm(z[:, 512 + hh * LANE:512 + (hh + 1) * LANE], gfq, LANE, HEAD_DIM)
        fq_out[:, hh * LANE:(hh + 1) * LANE] = (slot * HD_SCALE).astype(BF16)
    kpair = _seg_norm(z[:, 1536:1664], gfk_ref[...], 64, HEAD_DIM, HEAD_DIM)
    vpair = z[:, 1664:1792]
    fkv_out[:, 0:LANE] = kpair
    fkv_out[:, LANE:2 * LANE] = vpair
    fkvb_out[:, 0:LANE] = kpair.astype(BF16)
    fkvb_out[:, LANE:2 * LANE] = vpair.astype(BF16)
    ff = z[:, 1792:1920] + bf_ref[...]
    ls = jnp.minimum(ff, 0.0) - jnp.log1p(jnp.exp(-jnp.abs(ff)))
    lf_out[...] = ls[:, 0:N_HEADS]


def _proj_ab(x2, pos_tabs, w, tm):
    n, d = x2.shape
    cos, sin = pos_tabs
    ntab = cos.shape[0] // tm
    row = lambda width: pl.BlockSpec((tm, width), lambda i: (i, 0))
    tab = pl.BlockSpec((tm, LANE), lambda i: (i % ntab, 0))
    ins = [x2, w['g_mix'], w['w_in'], w['g_q_lora'], w['w_q_up'], w['g_q'], w['g_kpe'], w['g_kv_lora'],
           w['w_k'], w['w_v'], w['g_k_nope'], w['g_fox_q'], w['g_fox_k'], w['b_fox_f'], cos, sin]
    in_specs = [row(d)] + [_full(a.shape) for a in ins[1:14]] + [tab, tab]
    out_shape = [jax.ShapeDtypeStruct((n, 1024), BF16), jax.ShapeDtypeStruct((n, 160), F32),
                 jax.ShapeDtypeStruct((n, 1024), BF16), jax.ShapeDtypeStruct((n, 512), BF16),
                 jax.ShapeDtypeStruct((n, 1024), BF16), jax.ShapeDtypeStruct((n, 256), F32),
                 jax.ShapeDtypeStruct((n, 256), BF16), jax.ShapeDtypeStruct((n, N_HEADS), F32)]
    out_specs = [row(s.shape[1]) for s in out_shape]
    return pl.pallas_call(_proj_ab_body, grid=(n // tm,), in_specs=in_specs, out_specs=out_specs,
                          out_shape=out_shape, compiler_params=_params(("parallel",)),
                          name="proj_ab")(*ins)


def _proj_cd_body(with_ksum, tm, x_ref, gmix_ref, win_ref, gnq_ref, gks_ref, gkw_ref, bg_ref, gmq_ref, gmk_ref,
                  nq_out, cmp_out, sel_out, selb_out, win_out, winb_out, gate_out, mq_out, mqf_out,
                  mkv_out, mkvb_out, *maybe_ksum):
    h = _rms(x_ref[...], gmix_ref[...]).astype(BF16)
    z = _dot(h, win_ref[...])
    gnq = gnq_ref[...]
    for hh in range(N_HEADS):
        slot = _seg_norm(z[:, hh * LANE:(hh + 1) * LANE], gnq, LANE, HEAD_DIM)
        nq_out[:, hh * LANE:(hh + 1) * LANE] = (slot * HD_SCALE).astype(BF16)
    cmp_out[...] = z[:, 1024:1280]

    def kv_out(base, g_ref, f_ref, b_ref):
        kp = _seg_norm(z[:, base:base + LANE], g_ref[...], 64, HEAD_DIM, HEAD_DIM)
        vp = z[:, base + LANE:base + 2 * LANE]
        f_ref[:, 0:LANE] = kp
        f_ref[:, LANE:2 * LANE] = vp
        b_ref[:, 0:LANE] = kp.astype(BF16)
        b_ref[:, LANE:2 * LANE] = vp.astype(BF16)
        return kp

    kv_out(1280, gks_ref, sel_out, selb_out)
    kv_out(1536, gkw_ref, win_out, winb_out)
    gl = z[:, 1792:1920] + bg_ref[...]
    gate_out[...] = (1.0 / (1.0 + jnp.exp(-gl)))[:, 0:3 * N_HEADS]
    gmq = gmq_ref[...]
    for hh in range(N_HEADS):
        slot = _seg_norm(z[:, 1920 + hh * LANE:1920 + (hh + 1) * LANE], gmq, LANE, HEAD_DIM)
        mqf_out[:, hh * LANE:(hh + 1) * LANE] = slot
        mq_out[:, hh * LANE:(hh + 1) * LANE] = (slot * HD_SCALE).astype(BF16)
    kp = kv_out(2944, gmk_ref, mkv_out, mkvb_out)
    if with_ksum:
        ksum_out = maybe_ksum[0]
        for blk in range(tm // MOBA_BLOCK):
            ksum_out[blk] = jnp.sum(kp[blk * MOBA_BLOCK:(blk + 1) * MOBA_BLOCK], axis=0, keepdims=True) \
                * (1.0 / MOBA_BLOCK)


def _proj_cd(x2, w, tm, with_ksum):
    n, d = x2.shape
    row = lambda width: pl.BlockSpec((tm, width), lambda i: (i, 0))
    ins = [x2, w['g_mix'], w['w_in'], w['g_nsa_q'], w['g_nsa_ks'], w['g_nsa_kw'], w['b_gate'],
           w['g_moba_q'], w['g_moba_k']]
    in_specs = [row(d)] + [_full(a.shape) for a in ins[1:]]
    out_shape = [jax.ShapeDtypeStruct((n, 1024), BF16), jax.ShapeDtypeStruct((n, 256), F32),
                 jax.ShapeDtypeStruct((n, 256), F32), jax.ShapeDtypeStruct((n, 256), BF16),
                 jax.ShapeDtypeStruct((n, 256), F32), jax.ShapeDtypeStruct((n, 256), BF16),
                 jax.ShapeDtypeStruct((n, 3 * N_HEADS), F32), jax.ShapeDtypeStruct((n, 1024), BF16),
                 jax.ShapeDtypeStruct((n, 1024), F32), jax.ShapeDtypeStruct((n, 256), F32),
                 jax.ShapeDtypeStruct((n, 256), BF16)]
    out_specs = [row(s.shape[1]) for s in out_shape]
    if with_ksum:
        nb = tm // MOBA_BLOCK
        out_shape.append(jax.ShapeDtypeStruct((n // MOBA_BLOCK, 1, LANE), F32))
        out_specs.append(pl.BlockSpec((nb, 1, LANE), lambda i: (i, 0, 0)))
    return pl.pallas_call(functools.partial(_proj_cd_body, with_ksum, tm), grid=(n // tm,),
                          in_specs=in_specs, out_specs=out_specs, out_shape=out_shape,
                          compiler_params=_params(("parallel",)), name="proj_cd")(*ins)


def _post_body(n_a, x_ref, *refs):
    a_refs = refs[:n_a]
    b_ref, woa_ref, wob_ref, g_ref, wup_ref, wdn_ref, o_ref, x1_ref, h_ref, acc_ref = refs[n_a:]
    f = pl.program_id(1)

    @pl.when(f == 0)
    def _():
        a = a_refs[0][...]
        for r in a_refs[1:]:
            a = a + r[...]
        x1 = x_ref[...] + _dot(a.astype(BF16), woa_ref[...]) + _dot(b_ref[...].astype(BF16), wob_ref[...])
        x1_ref[...] = x1
        h_ref[...] = _rms(x1, g_ref[...]).astype(BF16)
        acc_ref[...] = jnp.zeros_like(acc_ref)

    u = _dot(h_ref[...], wup_ref[...])
    r = jnp.maximum(u, 0.0)
    acc_ref[...] += _dot((r * r).astype(BF16), wdn_ref[...])

    @pl.when(f == pl.num_programs(1) - 1)
    def _():
        o_ref[...] = x1_ref[...] + acc_ref[...]


def _post(x2, a_list, b, w_out, g_ffn, w_up, w_dn, tm):
    n, d = x2.shape
    dff = w_up.shape[1]
    tf = min(1024, dff)
    half = w_out.shape[0] // 2
    row = lambda width: pl.BlockSpec((tm, width), lambda i, f: (i, 0))
    in_specs = ([row(d)] + [row(half)] * (len(a_list) + 1)
                + [pl.BlockSpec((half, d), lambda i, f: (0, 0)), pl.BlockSpec((half, d), lambda i, f: (1, 0)),
                   pl.BlockSpec((1, d), lambda i, f: (0, 0)),
                   pl.BlockSpec((d, tf), lambda i, f: (0, f)), pl.BlockSpec((tf, d), lambda i, f: (f, 0))])
    return pl.pallas_call(
        functools.partial(_post_body, len(a_list)), grid=(n // tm, dff // tf), in_specs=in_specs,
        out_specs=row(d), out_shape=jax.ShapeDtypeStruct((n, d), F32),
        scratch_shapes=[pltpu.VMEM((tm, d), F32), pltpu.VMEM((tm, d), BF16), pltpu.VMEM((tm, d), F32)],
        compiler_params=_params(("parallel", "arbitrary")), name="post_mlp",
    )(x2, *a_list, b, w_out, w_out, g_ffn, w_up, w_dn)


def _cumsum_body(n_chunks, lf_ref, o_ref):
    def step(c, carry):
        off = pl.multiple_of(c * LANE, LANE)
        cum = _cumsum_lanes(lf_ref[:, pl.ds(off, LANE)]) + carry
        o_ref[:, pl.ds(off, LANE)] = cum
        return cum[:, LANE - 1:LANE]
    lax.fori_loop(0, n_chunks, step, jnp.zeros((N_HEADS, 1), F32))


def _cumsum_seq(lft):
    b, h, s = lft.shape
    spec = pl.BlockSpec((None, h, s), lambda i: (i, 0, 0))
    return pl.pallas_call(functools.partial(_cumsum_body, s // LANE), grid=(b,), in_specs=[spec],
                          out_specs=spec, out_shape=jax.ShapeDtypeStruct((b, h, s), F32),
                          compiler_params=_params(("parallel",)), name="fox_cumsum")(lft)


def _flash_body(kind, t, nkv, slopes, gate_idx, *refs):
    refs = list(refs)
    q_ref, k_ref, v_ref = refs[0:3]
    pos = 3
    cq_ref = ck_ref = e_ref = sb_ref = gate_ref = None
    if kind == 'fox':
        cq_ref, ck_ref = refs[pos:pos + 2]
        pos += 2
    if kind in ('sel', 'moba'):
        e_ref, sb_ref = refs[pos:pos + 2]
        pos += 2
    if gate_idx is not None:
        gate_ref = refs[pos]
        pos += 1
    o_ref, m_ref, l_ref, acc_ref = refs[pos:pos + 4]
    i = pl.program_id(1)
    jj = pl.program_id(2)
    banded = kind == 'win'
    j = i - (nkv - 1) + jj if banded else jj

    @pl.when(jj == 0)
    def _():
        m_ref[...] = jnp.full(m_ref.shape, NEG, F32)
        l_ref[...] = jnp.zeros(l_ref.shape, F32)
        acc_ref[...] = jnp.zeros(acc_ref.shape, F32)

    def step(masked):
        dist = (i * t + _iota((t, t), 0)) - (j * t + _iota((t, t), 1))
        distf = dist.astype(F32) if slopes is not None else None
        if masked:
            mask = (dist >= 0) & (dist < WINDOW) if banded else dist >= 0
        s_list, v_list = [], []
        for hh in range(N_HEADS):
            qh = q_ref[:, hh * LANE:(hh + 1) * LANE]
            if kind == 'mla':
                kh = k_ref[:, hh * LANE:(hh + 1) * LANE]
                vh = v_ref[:, (hh // 2) * LANE:(hh // 2 + 1) * LANE]
            else:
                kh = k_ref[...]
                vh = v_ref[...]
            if kind == 'sel':
                kvh = hh // GROUP
                qh = jnp.concatenate([qh, sb_ref[:, kvh * LANE:(kvh + 1) * LANE]], axis=1)
                kh = jnp.concatenate([kh, e_ref[...]], axis=1)
            if kind == 'moba':
                qh = jnp.concatenate([qh, sb_ref[:, hh * LANE:(hh + 1) * LANE]], axis=1)
                kh = jnp.concatenate([kh, e_ref[...]], axis=1)
            s = _dot_nt(qh, kh)
            if kind == 'fox':
                s = s + (cq_ref[:, hh:hh + 1] - ck_ref[hh:hh + 1, :])
            if slopes is not None:
                s = s - float(slopes[hh]) * distf
            if masked:
                s = jnp.where(mask, s, NEG)
            s_list.append(s)
            v_list.append(vh)
        prev = [(m_ref[hh], l_ref[hh], acc_ref[hh]) for hh in range(N_HEADS)]
        nxt = [_softmax_step(s_list[hh], v_list[hh], *prev[hh]) for hh in range(N_HEADS)]
        for hh in range(N_HEADS):
            m_ref[hh], l_ref[hh], acc_ref[hh] = nxt[hh]

    if banded:
        pl.when(j >= 0)(lambda: step(True))
    else:
        pl.when(j < i)(lambda: step(False))
        pl.when(j == i)(lambda: step(True))

    @pl.when(j == i)
    def _():
        for hh in range(N_HEADS):
            o = acc_ref[hh] / l_ref[hh]
            half = (hh % 2) if kind == 'mla' else (hh // GROUP)
            oh = o[:, half * HEAD_DIM:(half + 1) * HEAD_DIM]
            if gate_ref is not None:
                oh = oh * gate_ref[:, 3 * hh + gate_idx:3 * hh + gate_idx + 1]
            o_ref[:, hh * HEAD_DIM:(hh + 1) * HEAD_DIM] = oh


def _flash(kind, q, k_arr, v_arr, t, extra=(), gates=None, gate_idx=None, slopes=None):
    b, s, _ = q.shape
    nq = s // t
    banded = kind == 'win'
    nkv = ((WINDOW + t - 2) // t + 1) if banded else nq
    if banded:
        kvi = lambda i, jj: jnp.maximum(i - (nkv - 1) + jj, 0)
    else:
        kvi = lambda i, jj: jnp.minimum(jj, i)
    qspec = pl.BlockSpec((None, t, 1024), lambda bb, i, jj: (bb, i, 0))
    if kind == 'mla':
        kspec = pl.BlockSpec((None, t, 1024), lambda bb, i, jj: (bb, kvi(i, jj), 0))
        vspec = pl.BlockSpec((None, t, 512), lambda bb, i, jj: (bb, kvi(i, jj), 0))
    else:
        kspec = pl.BlockSpec((None, t, LANE), lambda bb, i, jj: (bb, kvi(i, jj), 0))
        vspec = pl.BlockSpec((None, t, LANE), lambda bb, i, jj: (bb, kvi(i, jj), 1))
    ins = [q, k_arr, v_arr]
    in_specs = [qspec, kspec, vspec]
    if kind == 'fox':
        cq, ckt = extra
        ins += [cq, ckt]
        in_specs += [pl.BlockSpec((None, t, N_HEADS), lambda bb, i, jj: (bb, i, 0)),
                     pl.BlockSpec((None, N_HEADS, t), lambda bb, i, jj: (bb, 0, kvi(i, jj)))]
    if kind in ('sel', 'moba'):
        e_tab, selb = extra
        ins += [e_tab, selb]
        in_specs += [pl.BlockSpec((t, LANE), lambda bb, i, jj: (kvi(i, jj), 0)),
                     pl.BlockSpec((None, t, selb.shape[2]), lambda bb, i, jj: (bb, i, 0))]
    if gates is not None:
        ins.append(gates)
        in_specs.append(pl.BlockSpec((None, t, 3 * N_HEADS), lambda bb, i, jj: (bb, i, 0)))
    return pl.pallas_call(
        functools.partial(_flash_body, kind, t, nkv, slopes, gate_idx),
        grid=(b, nq, nkv), in_specs=in_specs,
        out_specs=pl.BlockSpec((None, t, 512), lambda bb, i, jj: (bb, i, 0)),
        out_shape=jax.ShapeDtypeStruct((b, s, 512), F32),
        scratch_shapes=[pltpu.VMEM((N_HEADS, t, 1), F32), pltpu.VMEM((N_HEADS, t, 1), F32),
                        pltpu.VMEM((N_HEADS, t, LANE), F32)],
        compiler_params=_params(("parallel", "parallel", "arbitrary")), name="flash_" + kind,
    )(*ins)


def _compress_body(n_blk, paged, *refs):
    blk_refs = refs[:n_blk]
    (wka_ref, wkb_ref, wva_ref, wvb_ref, w2k_ref, w2v_ref, peka_ref, pekb_ref, peva_ref, pevb_ref,
     gkc_ref, kc_out, vc_out, ck_ref, cv_ref) = refs[n_blk:n_blk + 15]
    j = pl.program_id(1)

    @pl.when(j == 0)
    def _():
        ck_ref[...] = jnp.zeros_like(ck_ref)
        cv_ref[...] = jnp.zeros_like(cv_ref)

    if paged:
        t_refs = refs[n_blk + 15:]
        for pp in range(n_blk):
            pg = blk_refs[pp][...]
            t_refs[0][pp * PAGE:(pp + 1) * PAGE, :] = pg[0:LANE].T
            t_refs[1][pp * PAGE:(pp + 1) * PAGE, :] = pg[LANE:2 * LANE].T
        n = n_blk * PAGE // CMP_STRIDE

        def chunk_rows(off):
            t_ref = t_refs[off // LANE]
            return jnp.concatenate([t_ref[pl.ds(r, n, stride=CMP_STRIDE), :] for r in range(CMP_STRIDE)], axis=1)
    else:
        x = blk_refs[0][...]
        n = x.shape[0]

        def chunk_rows(off):
            return jnp.concatenate([x[:, r * 256 + off:r * 256 + off + LANE] for r in range(CMP_STRIDE)], axis=1)

    row = _iota((n, 2 * CMP_HIDDEN), 0)

    def stream(off, wa_ref, wb_ref, pea_ref, peb_ref, w2_ref, carry_ref):
        xs = chunk_rows(off)
        a = _dot((xs + pea_ref[...]).astype(BF16), wa_ref[...])
        bpart = _dot((xs + peb_ref[...]).astype(BF16), wb_ref[...])
        a_prev = jnp.where(row == 0, carry_ref[...], pltpu.roll(a, 1, 0))
        carry_ref[...] = a[n - 1:n, :]
        hid = a_prev + bpart
        act = hid * (1.0 / (1.0 + jnp.exp(-hid)))
        return _dot(act.astype(BF16), w2_ref[...])

    kc = stream(0, wka_ref, wkb_ref, peka_ref, pekb_ref, w2k_ref, ck_ref)
    kc_out[...] = _seg_norm(kc, gkc_ref[...], 64, HEAD_DIM, HEAD_DIM).astype(BF16)
    vc_out[...] = stream(LANE, wva_ref, wvb_ref, peva_ref, pevb_ref, w2v_ref, cv_ref).astype(BF16)


def _compress_weight_list(w):
    return [w['wka'], w['wkb'], w['wva'], w['wvb'], w['w2k'], w['w2v'],
            w['peka'], w['pekb'], w['peva'], w['pevb'], w['g_nsa_kc']]


def _compress_prompt(cmp_kv, w):
    b, s, _ = cmp_kv.shape
    rows = s // CMP_STRIDE
    rb = LANE
    x = cmp_kv.reshape(b, rows, CMP_STRIDE * 256)
    wl = _compress_weight_list(w)
    in_specs = [pl.BlockSpec((None, rb, CMP_STRIDE * 256), lambda bb, j: (bb, j, 0))] + [_full(a.shape) for a in wl]
    ospec = pl.BlockSpec((None, rb, LANE), lambda bb, j: (bb, j, 0))
    osh = jax.ShapeDtypeStruct((b, rows, LANE), BF16)
    return pl.pallas_call(
        functools.partial(_compress_body, 1, False), grid=(b, rows // rb), in_specs=in_specs,
        out_specs=[ospec, ospec], out_shape=[osh, osh],
        scratch_shapes=[pltpu.VMEM((1, 2 * CMP_HIDDEN), F32), pltpu.VMEM((1, 2 * CMP_HIDDEN), F32)],
        compiler_params=_params(("parallel", "arbitrary")), name="compress_prompt")(x, *wl)


def _compress_sample(cache, li, page_table, w, p):
    nseq, n_pages = page_table.shape
    rp = PAGE // CMP_STRIDE
    wl = _compress_weight_list(w)
    in_specs = [_page_spec(li, 256, p, pp) for pp in range(p)] + [_const_spec(a) for a in wl]
    ospec = pl.BlockSpec((None, rp * p, LANE), lambda bb, j, pt: (bb, j, 0))
    osh = jax.ShapeDtypeStruct((nseq, rp * n_pages, LANE), BF16)
    gs = pltpu.PrefetchScalarGridSpec(
        num_scalar_prefetch=1, grid=(nseq, n_pages // p), in_specs=in_specs, out_specs=[ospec, ospec],
        scratch_shapes=[pltpu.VMEM((1, 2 * CMP_HIDDEN), F32), pltpu.VMEM((1, 2 * CMP_HIDDEN), F32),
                        pltpu.VMEM((p * PAGE, LANE), F32), pltpu.VMEM((p * PAGE, LANE), F32)])
    return pl.pallas_call(
        lambda pt, *r: _compress_body(p, True, *r), grid_spec=gs, out_shape=[osh, osh],
        compiler_params=_params(("parallel", "arbitrary")), name="compress_sample",
    )(page_table, *([cache] * p), *wl)


def _cmp_select_body(t, n_cmp, slopes, q_ref, kc_ref, vc_ref, ov_ref, gate_ref, o_ref, sb_ref):
    i = pl.program_id(1)
    ncp = kc_ref.shape[0]
    qpos = i * t + _iota((t, 1), 0)
    ridx = _iota((1, ncp), 1)
    cmp_end = ridx * CMP_STRIDE + (CMP_STRIDE - 1)
    valid = (ridx >= 1) & (ridx <= n_cmp) & (cmp_end <= qpos)
    dist = (qpos - cmp_end).astype(F32)
    kc, vc, ov = kc_ref[...], vc_ref[...], ov_ref[...]
    blk = _iota((t, LANE), 1)
    cur = qpos // SEL_BLOCK
    started = blk * SEL_BLOCK <= qpos
    forced = (blk == 0) | (blk == cur) | (blk == cur - 1)
    for kvh in range(N_KV):
        psum = jnp.zeros((t, ncp), F32)
        for g in range(GROUP):
            hh = kvh * GROUP + g
            s = _dot_nt(q_ref[:, hh * LANE:(hh + 1) * LANE], kc) - float(slopes[hh]) * dist
            s = jnp.where(valid, s, NEG)
            m = jnp.max(s, axis=-1, keepdims=True)
            p = jnp.where(valid, jnp.exp(s - m), 0.0)
            p = p / jnp.maximum(jnp.sum(p, axis=-1, keepdims=True), 1e-30)
            o = _dot(p.astype(BF16), vc)[:, kvh * HEAD_DIM:(kvh + 1) * HEAD_DIM]
            o_ref[:, hh * HEAD_DIM:(hh + 1) * HEAD_DIM] = o * gate_ref[:, 3 * hh:3 * hh + 1]
            psum = psum + p
        hi, lo = _split2(psum)
        imp = _dot(hi, ov) + _dot(lo, ov)
        sc = jnp.where(started, jnp.where(forced, -NEG, imp), NEG)
        sel = _topk_select(sc, SEL_TOPN)
        sb_ref[:, kvh * LANE:(kvh + 1) * LANE] = jnp.where(sel > 0.5, 0.0, BLOCK_NEG).astype(BF16)


def _cmp_select(nq, kc, vc, ov, gates, t, slopes):
    b, s, _ = nq.shape
    ncp = kc.shape[1]
    n_cmp = (s - CMP_LEN) // CMP_STRIDE + 1
    return pl.pallas_call(
        functools.partial(_cmp_select_body, t, n_cmp, slopes), grid=(b, s // t),
        in_specs=[pl.BlockSpec((None, t, 1024), lambda bb, i: (bb, i, 0)),
                  pl.BlockSpec((None, ncp, LANE), lambda bb, i: (bb, 0, 0)),
                  pl.BlockSpec((None, ncp, LANE), lambda bb, i: (bb, 0, 0)),
                  pl.BlockSpec(ov.shape, lambda bb, i: (0, 0)),
                  pl.BlockSpec((None, t, 3 * N_HEADS), lambda bb, i: (bb, i, 0))],
        out_specs=[pl.BlockSpec((None, t, 512), lambda bb, i: (bb, i, 0)),
                   pl.BlockSpec((None, t, 2 * LANE), lambda bb, i: (bb, i, 0))],
        out_shape=[jax.ShapeDtypeStruct((b, s, 512), F32), jax.ShapeDtypeStruct((b, s, 2 * LANE), BF16)],
        compiler_params=_params(("parallel", "parallel")), name="nsa_cmp_select")(nq, kc, vc, ov, gates)


def _moba_select_body(t, qf_ref, km_ref, sb_ref):
    i = pl.program_id(1)
    qpos = i * t + _iota((t, 1), 0)
    own = qpos // MOBA_BLOCK
    blk = _iota((t, LANE), 1)
    kh, kl = _split2(km_ref[...])
    for hh in range(N_HEADS):
        qh, ql = _split2(qf_ref[:, hh * LANE:(hh + 1) * LANE])
        sc = _dot_nt(qh, kh) + (_dot_nt(qh, kl) + _dot_nt(ql, kh))
        sel = _topk_select(jnp.where(blk < own, sc, NEG), MOBA_TOPK)
        keep = (sel > 0.5) | (blk == own)
        sb_ref[:, hh * LANE:(hh + 1) * LANE] = jnp.where(keep, 0.0, BLOCK_NEG).astype(BF16)


def _moba_select(mqf, kmean, t):
    b, s, _ = mqf.shape
    return pl.pallas_call(
        functools.partial(_moba_select_body, t), grid=(b, s // t),
        in_specs=[pl.BlockSpec((None, t, 1024), lambda bb, i: (bb, i, 0)),
                  pl.BlockSpec((None, LANE, LANE), lambda bb, i: (bb, 0, 0))],
        out_specs=pl.BlockSpec((None, t, 1024), lambda bb, i: (bb, i, 0)),
        out_shape=jax.ShapeDtypeStruct((b, s, 1024), BF16),
        compiler_params=_params(("parallel", "parallel")), name="moba_select")(mqf, kmean)


def _row_query(shape, n_heads=N_HEADS):
    return _iota(shape, 0) // n_heads


def _row_kv_is1(shape):
    return (_iota(shape, 0) % N_HEADS) >= GROUP


def _pick_half(acc):
    r = acc.shape[0]
    return jnp.where(_row_kv_is1((r, HEAD_DIM)), acc[:, HEAD_DIM:2 * HEAD_DIM], acc[:, 0:HEAD_DIM])


def _page_spec(li, n_feat, p, pp):
    return pl.BlockSpec((None, None, n_feat, PAGE), lambda bb, j, pt: (li, pt[bb, j * p + pp], 0, 0))


def _cat_pages(refs):
    if len(refs) == 1:
        return refs[0][...]
    return jnp.concatenate([r[...] for r in refs], axis=1)


def _const_spec(a):
    nd = a.ndim
    return pl.BlockSpec(a.shape, lambda bb, j, pt: (0,) * nd)


def _seq_spec(shape_tail):
    nd = len(shape_tail)
    return pl.BlockSpec((None,) + tuple(shape_tail), lambda bb, j, pt: (bb,) + (0,) * nd)


def _sample_init(j, m_ref, l_ref, acc_ref):
    @pl.when(j == 0)
    def _():
        m_ref[...] = jnp.full(m_ref.shape, NEG, F32)
        l_ref[...] = jnp.zeros(l_ref.shape, F32)
        acc_ref[...] = jnp.zeros(acc_ref.shape, F32)


def _chain_scratch(rows):
    return [pltpu.VMEM((N_CHAINS, rows, 1), F32), pltpu.VMEM((N_CHAINS, rows, 1), F32),
            pltpu.VMEM((N_CHAINS, rows, LANE), F32)]


def _new_rows_mask(rows, n_new):
    col = _iota((rows, PAGE), 1)
    return (col <= _row_query((rows, PAGE))) & (col < n_new)


def _s_fox_body(p, n_new, pt_ref, q_ref, *refs):
    kv_refs = refs[:p]
    lf_refs = refs[p:2 * p]
    nkv_ref, nlf_ref, o_ref, m_ref, l_ref, acc_ref, carry_ref = refs[2 * p:]
    j = pl.program_id(1)
    _sample_init(j, m_ref, l_ref, acc_ref)

    @pl.when(j == 0)
    def _():
        carry_ref[...] = jnp.zeros_like(carry_ref)

    q = q_ref[...]
    reps = q.shape[0] // N_HEADS
    carry = carry_ref[...]
    local = [_cumsum_lanes(lf_refs[pp][...]) for pp in range(p)]
    cums = []
    for pp in range(p):
        cums.append(local[pp] + carry)
        carry = carry + local[pp][:, PAGE - 1:PAGE]
    carry_ref[...] = carry
    ppc = p // N_CHAINS
    s_list, v_list = [], []
    for c in range(N_CHAINS):
        kv = _cat_pages(kv_refs[c * ppc:(c + 1) * ppc])
        cum_c = jnp.concatenate(cums[c * ppc:(c + 1) * ppc], axis=1) if ppc > 1 else cums[c]
        s_list.append(_dot(q, kv[0:LANE].astype(BF16)) - jnp.concatenate([cum_c] * reps, axis=0))
        v_list.append(kv[LANE:2 * LANE].astype(BF16))
    _chains_update(s_list, v_list, m_ref, l_ref, acc_ref, vt=True)

    @pl.when(j == pl.num_programs(1) - 1)
    def _():
        cum = _cumsum_lanes(nlf_ref[...]) + carry
        nkv = nkv_ref[...]
        sn = _dot(q, nkv[0:LANE].astype(BF16)) - jnp.concatenate([cum] * reps, axis=0)
        sn = jnp.where(_new_rows_mask(sn.shape[0], n_new), sn, NEG)
        _chains_update([sn], [nkv[LANE:2 * LANE].astype(BF16)], m_ref, l_ref, acc_ref, vt=True)
        o_ref[...] = _pick_half(_chains_merge(m_ref, l_ref, acc_ref))


def _s_fox(q, cache_kv, cache_lft, li, page_table, new_kv, new_lft, p, n_new):
    nseq, n_pages = page_table.shape
    rows = q.shape[1]
    in_specs = ([_seq_spec((rows, LANE))] + [_page_spec(li, 256, p, pp) for pp in range(p)]
                + [pl.BlockSpec((None, None, N_HEADS, PAGE),
                                functools.partial(lambda bb, j, pt, pp: (li, pt[bb, j * p + pp], 0, 0), pp=pp))
                   for pp in range(p)]
                + [_seq_spec((256, PAGE)), _seq_spec((N_HEADS, PAGE))])
    gs = pltpu.PrefetchScalarGridSpec(
        num_scalar_prefetch=1, grid=(nseq, n_pages // p), in_specs=in_specs,
        out_specs=_seq_spec((rows, HEAD_DIM)),
        scratch_shapes=_chain_scratch(rows) + [pltpu.VMEM((N_HEADS, 1), F32)])
    return pl.pallas_call(
        functools.partial(_s_fox_body, p, n_new), grid_spec=gs,
        out_shape=jax.ShapeDtypeStruct((nseq, rows, HEAD_DIM), F32),
        compiler_params=_params(("parallel", "arbitrary")), name="sample_fox",
    )(page_table, q, *([cache_kv] * p), *([cache_lft] * p), new_kv, new_lft)


def _s_mla_body(p, n_new, pt_ref, q_ref, *refs):
    lat_refs = refs[:p]
    (nlat_ref, wkslot_ref, gkn_ref, wkct_ref, wv_ref, o_ref,
     m_ref, l_ref, acc_ref, qa_ref, qr_ref) = refs[p:]
    j = pl.program_id(1)
    _sample_init(j, m_ref, l_ref, acc_ref)
    rows = q_ref.shape[0]
    head = _iota((rows, LANE), 0) % N_HEADS

    @pl.when(j == 0)
    def _():
        q = q_ref[...].astype(F32)
        qa = jnp.zeros((rows, LANE), F32)
        qr = jnp.zeros((rows, LANE), F32)
        for hh in range(N_HEADS):
            slot = q[:, hh * LANE:(hh + 1) * LANE]
            qn = (slot * gkn_ref[...]).astype(BF16)
            qa = jnp.where(head == hh, _dot_nt(qn, wkslot_ref[:, hh * LANE:(hh + 1) * LANE]), qa)
            qr = jnp.where(head == hh, pltpu.roll(slot, 64, 1), qr)
        qa_ref[...] = qa.astype(BF16)
        lane = _iota((rows, LANE), 1)
        qr_ref[...] = jnp.where(lane < MLA_ROPE, qr, 0.0).astype(BF16)

    qa, qr = qa_ref[...], qr_ref[...][:, 0:MLA_ROPE]
    wkct = wkct_ref[...]

    def logits(lat):
        keys = lat.shape[1]
        latc = lat[0:MLA_KV_LORA].astype(BF16)
        latr = lat[MLA_KV_LORA:MLA_KV_LORA + MLA_ROPE].astype(BF16)
        kraw = _dot(wkct, latc)
        ms = jnp.sum((kraw * kraw).reshape(N_HEADS, MLA_NOPE, keys), axis=1) * (1.0 / MLA_NOPE)
        r = lax.rsqrt(ms + NORM_EPS)
        return _dot(qa, latc) * jnp.concatenate([r] * (rows // N_HEADS), axis=0) + _dot(qr, latr), latc

    ppc = p // N_CHAINS
    parts = [logits(_cat_pages(lat_refs[c * ppc:(c + 1) * ppc])) for c in range(N_CHAINS)]
    _chains_update([s for s, _ in parts], [lc for _, lc in parts], m_ref, l_ref, acc_ref, vt=True)

    @pl.when(j == pl.num_programs(1) - 1)
    def _():
        sn, latn = logits(nlat_ref[...])
        sn = jnp.where(_new_rows_mask(rows, n_new), sn, NEG)
        _chains_update([sn], [latn], m_ref, l_ref, acc_ref, vt=True)
        olat = _chains_merge(m_ref, l_ref, acc_ref).astype(BF16)
        hd = _iota((rows, MLA_V), 0) % N_HEADS
        o = jnp.zeros((rows, MLA_V), F32)
        for hh in range(N_HEADS):
            o = jnp.where(hd == hh, _dot(olat, wv_ref[:, hh * MLA_V:(hh + 1) * MLA_V]), o)
        o_ref[...] = o


def _s_mla(q, cache, li, page_table, new_lat, w, p, n_new):
    nseq, n_pages = page_table.shape
    rows = q.shape[1]
    consts = [w['w_k'], w['g_k_nope'], w['w_k_ct'], w['w_v']]
    n_lat = MLA_KV_LORA + MLA_ROPE
    in_specs = ([_seq_spec((rows, N_HEADS * LANE))] + [_page_spec(li, n_lat, p, pp) for pp in range(p)]
                + [_seq_spec((n_lat, PAGE))] + [_const_spec(a) for a in consts])
    gs = pltpu.PrefetchScalarGridSpec(
        num_scalar_prefetch=1, grid=(nseq, n_pages // p), in_specs=in_specs,
        out_specs=_seq_spec((rows, MLA_V)),
        scratch_shapes=_chain_scratch(rows) + [pltpu.VMEM((rows, LANE), BF16), pltpu.VMEM((rows, LANE), BF16)])
    return pl.pallas_call(
        functools.partial(_s_mla_body, p, n_new), grid_spec=gs,
        out_shape=jax.ShapeDtypeStruct((nseq, rows, MLA_V), F32),
        compiler_params=_params(("parallel", "arbitrary")), name="sample_mla",
    )(page_table, q, *([cache] * p), new_lat, *consts)


def _s_block_body(p, n_new, past_len, gate_idx, pt_ref, q_ref, sb_ref, sbn_ref, e_ref, slope_ref, *refs):
    kv_refs = refs[:p]
    rest = list(refs[p:])
    nkv_ref = rest.pop(0)
    gate_ref = rest.pop(0) if gate_idx is not None else None
    o_ref, m_ref, l_ref, acc_ref = rest
    j = pl.program_id(1)
    _sample_init(j, m_ref, l_ref, acc_ref)
    q = q_ref[...]
    rows = q.shape[0]
    slope = slope_ref[...]
    qpos = past_len + _row_query((rows, 1))
    kpos = j * (p * PAGE) + _iota((1, p * PAGE), 1)
    q_aug = jnp.concatenate([q, sb_ref[...]], axis=1)
    e_c = _lane_groups(e_ref[...], N_CHAINS)
    ppc = p // N_CHAINS
    bias_c = _lane_groups(-slope * (qpos - kpos).astype(F32), N_CHAINS)
    s_list, v_list = [], []
    for c in range(N_CHAINS):
        kv = _cat_pages(kv_refs[c * ppc:(c + 1) * ppc])
        k_aug = jnp.concatenate([kv[0:LANE].astype(BF16), e_c[c]], axis=0)
        s_list.append(_dot(q_aug, k_aug) + bias_c[c])
        v_list.append(kv[LANE:2 * LANE].astype(BF16))
    _chains_update(s_list, v_list, m_ref, l_ref, acc_ref, vt=True)

    @pl.when(j == pl.num_programs(1) - 1)
    def _():
        nkv = nkv_ref[...]
        npos = past_len + _iota((1, PAGE), 1)
        sn = (_dot(q, nkv[0:LANE].astype(BF16)) + sbn_ref[...][:, 0:1].astype(F32)
              - slope * (qpos - npos).astype(F32))
        sn = jnp.where(_new_rows_mask(rows, n_new), sn, NEG)
        _chains_update([sn], [nkv[LANE:2 * LANE].astype(BF16)], m_ref, l_ref, acc_ref, vt=True)
        o = _pick_half(_chains_merge(m_ref, l_ref, acc_ref))
        if gate_ref is not None:
            o = o * gate_ref[:, gate_idx:gate_idx + 1]
        o_ref[...] = o


def _s_block(q, sb_chunks, sb_new, e_mat, slope_rows, cache, li, page_table, new_kv, p, n_new, gates=None,
             gate_idx=None):
    nseq, n_pages = page_table.shape
    rows = q.shape[1]
    past_len = n_pages * PAGE
    in_specs = ([_seq_spec((rows, LANE)),
                 pl.BlockSpec((None, None, rows, LANE), lambda bb, j, pt: (bb, j, 0, 0)),
                 _seq_spec((rows, LANE)), _const_spec(e_mat), _const_spec(slope_rows)]
                + [_page_spec(li, 256, p, pp) for pp in range(p)] + [_seq_spec((256, PAGE))])
    ins = [q, sb_chunks, sb_new, e_mat, slope_rows] + [cache] * p + [new_kv]
    if gates is not None:
        in_specs.append(_seq_spec((rows, 3)))
        ins.append(gates)
    gs = pltpu.PrefetchScalarGridSpec(
        num_scalar_prefetch=1, grid=(nseq, n_pages // p), in_specs=in_specs,
        out_specs=_seq_spec((rows, HEAD_DIM)), scratch_shapes=_chain_scratch(rows))
    return pl.pallas_call(
        functools.partial(_s_block_body, p, n_new, past_len, gate_idx), grid_spec=gs,
        out_shape=jax.ShapeDtypeStruct((nseq, rows, HEAD_DIM), F32),
        compiler_params=_params(("parallel", "arbitrary")), name="sample_block_attn",
    )(page_table, *ins)


def _s_cmp_body(n_new, past_len, q_ref, kc_ref, vc_ref, ov_ref, grp_ref, grpt_ref, slope_ref, gate_ref,
                o_ref, sb_ref):
    q = q_ref[...]
    rows = q.shape[0]
    ncp = kc_ref.shape[0]
    qpos = past_len + _row_query((rows, 1))
    ridx = _iota((1, ncp), 1)
    cmp_end = ridx * CMP_STRIDE + (CMP_STRIDE - 1)
    valid = (ridx >= 1) & (cmp_end <= qpos)
    s = _dot_nt(q, kc_ref[...]) - slope_ref[...] * (qpos - cmp_end).astype(F32)
    s = jnp.where(valid, s, NEG)
    m = jnp.max(s, axis=-1, keepdims=True)
    pr = jnp.where(valid, jnp.exp(s - m), 0.0)
    pr = pr / jnp.maximum(jnp.sum(pr, axis=-1, keepdims=True), 1e-30)
    o_ref[...] = _pick_half(_dot(pr.astype(BF16), vc_ref[...])) * gate_ref[:, 0:1]
    hi, lo = _split2(pr)
    grp = grp_ref[...]
    psum = _dot(grp, hi) + _dot(grp, lo)
    hi, lo = _split2(psum)
    imp = _dot(hi, ov_ref[...]) + _dot(lo, ov_ref[...])
    g_rows = imp.shape[0]
    gq = past_len + _iota(imp.shape, 0) // N_KV
    blk = _iota(imp.shape, 1)
    cur = gq // SEL_BLOCK
    started = blk * SEL_BLOCK <= gq
    forced = (blk == 0) | (blk == cur) | (blk == cur - 1)
    sel = _topk_select(jnp.where(started, jnp.where(forced, -NEG, imp), NEG), SEL_TOPN)
    sbg = jnp.where(sel > 0.5, 0.0, BLOCK_NEG).astype(BF16)
    sb_ref[...] = _dot(grpt_ref[...], sbg).astype(BF16)


def _s_cmp(q, kc, vc, ov, grp, grpt, slope_rows, gates, n_new, past_len):
    nseq, rows, _ = q.shape
    ncp = kc.shape[1]
    nsbp = ov.shape[1]
    seq = lambda tail: pl.BlockSpec((None,) + tuple(tail), lambda bb: (bb,) + (0,) * len(tail))
    cst = lambda a: pl.BlockSpec(a.shape, lambda bb: (0,) * a.ndim)
    return pl.pallas_call(
        functools.partial(_s_cmp_body, n_new, past_len), grid=(nseq,),
        in_specs=[seq((rows, LANE)), seq((ncp, LANE)), seq((ncp, LANE)), cst(ov), cst(grp), cst(grpt),
                  cst(slope_rows), seq((rows, 3))],
        out_specs=[seq((rows, HEAD_DIM)), seq((rows, nsbp))],
        out_shape=[jax.ShapeDtypeStruct((nseq, rows, HEAD_DIM), F32),
                   jax.ShapeDtypeStruct((nseq, rows, nsbp), BF16)],
        compiler_params=_params(("parallel",)), name="sample_cmp_select")(q, kc, vc, ov, grp, grpt, slope_rows, gates)


def _s_moba_score_body(p, own_blk, pt_ref, qf_ref, *refs):
    k_refs = refs[:p]
    sb_ref, sc_ref = refs[p:]
    j = pl.program_id(1)
    rows = qf_ref.shape[0]
    lane = _iota((rows, LANE), 1)

    @pl.when(j == 0)
    def _():
        sc_ref[...] = jnp.full(sc_ref.shape, NEG, F32)

    qh, ql = _split2(qf_ref[...])
    ppb = MOBA_BLOCK // PAGE
    sc = sc_ref[...]
    for b in range(p // ppb):
        kh, kl = _split2(_cat_pages(k_refs[b * ppb:(b + 1) * ppb]))
        qk = _dot(qh, kh) + (_dot(qh, kl) + _dot(ql, kh))
        val = jnp.sum(qk, axis=-1, keepdims=True) * (1.0 / MOBA_BLOCK)
        sc = jnp.where(lane == j * (p // ppb) + b, val, sc)
    sc_ref[...] = sc

    @pl.when(j == pl.num_programs(1) - 1)
    def _():
        sel = _topk_select(jnp.where(lane < own_blk, sc, NEG), MOBA_TOPK)
        keep = (sel > 0.5) | (lane == own_blk)
        sb_ref[...] = jnp.where(keep, 0.0, BLOCK_NEG).astype(BF16)


def _s_moba_score(qf, cache, li, page_table, p):
    nseq, n_pages = page_table.shape
    rows = qf.shape[1]
    own_blk = (n_pages * PAGE) // MOBA_BLOCK
    in_specs = [_seq_spec((rows, LANE))] + [_page_spec(li, LANE, p, pp) for pp in range(p)]
    gs = pltpu.PrefetchScalarGridSpec(
        num_scalar_prefetch=1, grid=(nseq, n_pages // p), in_specs=in_specs,
        out_specs=_seq_spec((rows, LANE)), scratch_shapes=[pltpu.VMEM((rows, LANE), F32)])
    return pl.pallas_call(
        functools.partial(_s_moba_score_body, p, own_blk), grid_spec=gs,
        out_shape=jax.ShapeDtypeStruct((nseq, rows, LANE), BF16),
        compiler_params=_params(("parallel", "arbitrary")), name="sample_moba_score",
    )(page_table, qf, *([cache] * p))


def _s_win_body(n_new, past_len, q_ref, st_ref, nkv_ref, slope_ref, gate_ref, o_ref):
    q = q_ref[...]
    rows = q.shape[0]
    wb = st_ref.shape[1]
    slope = slope_ref[...]
    qpos = past_len + _row_query((rows, 1))
    st = st_ref[...]
    wpos = past_len - wb + _iota((1, wb), 1)
    d1 = qpos - wpos
    s1 = _dot(q, st[0:LANE].astype(BF16)) - slope * d1.astype(F32)
    s1 = jnp.where((d1 >= 0) & (d1 < WINDOW) & (wpos >= 0), s1, NEG)
    nkv = nkv_ref[...]
    d2 = qpos - (past_len + _iota((1, PAGE), 1))
    s2 = _dot(q, nkv[0:LANE].astype(BF16)) - slope * d2.astype(F32)
    s2 = jnp.where(_new_rows_mask(rows, n_new), s2, NEG)
    m = jnp.maximum(jnp.max(s1, axis=-1, keepdims=True), jnp.max(s2, axis=-1, keepdims=True))
    p1, p2 = jnp.exp(s1 - m), jnp.exp(s2 - m)
    den = jnp.sum(p1, axis=-1, keepdims=True) + jnp.sum(p2, axis=-1, keepdims=True)
    acc = _dot_nt(p1.astype(BF16), st[LANE:2 * LANE].astype(BF16)) \
        + _dot_nt(p2.astype(BF16), nkv[LANE:2 * LANE].astype(BF16))
    o_ref[...] = _pick_half(acc / den) * gate_ref[:, 2:3]


def _s_win(q, state, li, new_kv, slope_rows, gates, n_new, past_len):
    nseq, rows, _ = q.shape
    wb = state.shape[3]
    seq = lambda tail: pl.BlockSpec((None,) + tuple(tail), lambda bb: (bb,) + (0,) * len(tail))
    return pl.pallas_call(
        functools.partial(_s_win_body, n_new, past_len), grid=(nseq,),
        in_specs=[seq((rows, LANE)), pl.BlockSpec((None, None, 256, wb), lambda bb: (li, bb, 0, 0)),
                  seq((256, PAGE)), pl.BlockSpec(slope_rows.shape, lambda bb: (0, 0)), seq((rows, 3))],
        out_specs=seq((rows, HEAD_DIM)),
        out_shape=jax.ShapeDtypeStruct((nseq, rows, HEAD_DIM), F32),
        compiler_params=_params(("parallel",)), name="sample_window")(q, state, new_kv, slope_rows, gates)


def _onehot_blocks(n_keys, block, width=LANE):
    e = (np.arange(n_keys)[:, None] // block) == np.arange(width)[None, :]
    return jnp.asarray(e, BF16)


def _overlap_rows(n_rows, n_cols):
    i0 = (np.arange(n_rows)[:, None] - 1) * CMP_STRIDE
    j0 = np.arange(n_cols)[None, :] * SEL_BLOCK
    ov = (i0 < j0 + SEL_BLOCK) & (i0 + CMP_LEN > j0) & (np.arange(n_rows)[:, None] >= 1)
    return jnp.asarray(ov, BF16)


def _tiles(n_rows, prefer):
    for t in (prefer, 512, 256, 128, 64, 32, 16, 8):
        if t <= n_rows and n_rows % t == 0:
            return t
    return n_rows


def _sample_rows(a, nseq, n_tok):
    return a.reshape(nseq, n_tok * N_HEADS, LANE)


def _pad_new(a, nseq, n_tok):
    a = a.reshape(nseq, n_tok, a.shape[-1])
    return jnp.swapaxes(jnp.pad(a, ((0, 0), (0, PAGE - n_tok), (0, 0))), 1, 2)


def _feat_major(c):
    nd = c.ndim
    t = jnp.transpose(c, tuple(range(nd - 4)) + (nd - 3, nd - 2, nd - 1, nd - 4))
    return t.reshape(t.shape[:nd - 4] + (2 * N_KV * HEAD_DIM, c.shape[nd - 4]))


def _rows_out(o, nseq, n_tok):
    return o.reshape(nseq * n_tok, N_HEADS * HEAD_DIM)


def kernel(x_prompt, x_sample, cache_mla, cache_fox_kv, cache_fox_logf, cache_nsa_cmp_kv, cache_nsa_sel_kv,
           state_nsa_win_kv, cache_moba_kv, page_table, g_mix, g_ffn, w_ffn_up, w_ffn_down, w_in_ab, w_out_ab,
           g_mla_q_lora, w_mla_q_up, g_mla_q_nope, g_mla_q_rope, g_mla_kv_lora, g_mla_k_rope, w_mla_kv_up,
           g_mla_k_nope, g_fox_q, g_fox_k, b_fox_f, w_in_cd, w_out_cd, g_nsa_q, g_nsa_kc, g_nsa_ks, g_nsa_kw,
           nsa_pe_k, nsa_pe_v, w_nsa_ck1, w_nsa_ck2, w_nsa_cv1, w_nsa_cv2, b_nsa_gate, g_moba_q, g_moba_k):
    nb, seq, d = x_prompt.shape
    nseq, n_tok, _ = x_sample.shape
    n_pages = page_table.shape[1]
    past_len = n_pages * PAGE
    depth = g_mix.shape[0]
    assert seq % 2048 == 0 and seq // SEL_BLOCK <= LANE and n_pages % 4 == 0 and n_tok * N_HEADS <= 32
    rows = n_tok * N_HEADS
    t_attn = 512
    t_sel = 256
    tm_p = _tiles(nb * seq, 512)
    tm_s = _tiles(nseq * n_tok, 512)
    p_pages = 16 if n_pages % 16 == 0 and n_pages >= 32 else 4
    slopes_n, slopes_m = _alibi_slopes()

    xp = x_prompt.reshape(nb * seq, d)
    xs = x_sample.reshape(nseq * n_tok, d)
    pos_p = _rope_tables(jnp.arange(seq))
    pos_s_tab = _rope_tables(past_len + jnp.arange(n_tok))
    reps = tm_s // n_tok
    pos_s = (jnp.tile(pos_s_tab[0], (reps, 1)), jnp.tile(pos_s_tab[1], (reps, 1)))

    qrow = np.arange(rows) // N_HEADS
    hrow = np.arange(rows) % N_HEADS
    slope_rows_n = jnp.asarray(slopes_n[hrow].reshape(rows, 1), F32)
    slope_rows_m = jnp.asarray(slopes_m[hrow].reshape(rows, 1), F32)
    n_grp = n_tok * N_KV
    grp_np = ((qrow * N_KV + hrow // GROUP)[None, :] == np.arange(n_grp)[:, None])
    grp = jnp.asarray(grp_np, BF16)
    grpt = jnp.asarray(grp_np.T, BF16)
    e_sel_s = _onehot_blocks(p_pages * PAGE, SEL_BLOCK).T
    e_mb_s = _onehot_blocks(p_pages * PAGE, MOBA_BLOCK).T
    nsb_s = -(-(past_len + n_tok) // SEL_BLOCK)
    nsbp_s = -(-nsb_s // LANE) * LANE
    ov_s = _overlap_rows(past_len // CMP_STRIDE, nsbp_s)
    e_sel_p = _onehot_blocks(seq, SEL_BLOCK)
    e_mb_p = _onehot_blocks(seq, MOBA_BLOCK)
    ov_p = _overlap_rows(seq // CMP_STRIDE, LANE)

    lists = [[] for _ in range(14)]
    (mla_p, mla_s, fkv_p, fkv_s, flf_p, flf_s, cmp_p, cmp_s, sel_p, sel_s, win_p, win_s, mb_p, mb_s) = lists

    for layer in range(depth):
        li = layer // 2
        gm = g_mix[layer].reshape(1, d)
        if layer % 2 == 0:
            w = dict(
                g_mix=gm, w_in=_take_cols(w_in_ab[li], _ab_col_index()).astype(BF16),
                g_q_lora=g_mla_q_lora[li].reshape(1, -1),
                w_q_up=_take_cols(w_mla_q_up[li], _slots_idx(0, N_HEADS, MLA_NOPE + MLA_ROPE, lambda h: 0)).astype(BF16),
                g_q=jnp.concatenate([_pad_lanes(g_mla_q_nope[li], 0, 64), _pad_lanes(g_mla_q_rope[li], 0, 64)], axis=1),
                g_kpe=_pad_lanes(g_mla_k_rope[li], 64),
                g_kv_lora=g_mla_kv_lora[li].reshape(1, -1),
                g_k_nope=_pad_lanes(g_mla_k_nope[li], 0),
                g_fox_q=jnp.tile(g_fox_q[li].reshape(1, -1), (1, 2)),
                g_fox_k=jnp.tile(g_fox_k[li].reshape(1, -1), (1, 2)),
                b_fox_f=_pad_lanes(b_fox_f[li], 0))
            wkv = w_mla_kv_up[li].reshape(MLA_KV_LORA, N_HEADS, MLA_NOPE + MLA_V)
            wk_c = wkv[:, :, :MLA_NOPE].reshape(MLA_KV_LORA, N_HEADS * MLA_NOPE)
            w['w_k'] = _take_cols(wk_c, _slots_idx(0, N_HEADS, MLA_NOPE, lambda h: 0)).astype(BF16)
            w['w_v'] = wkv[:, :, MLA_NOPE:].reshape(MLA_KV_LORA, N_HEADS * MLA_V).astype(BF16)
            w['w_k_ct'] = wk_c.T.astype(BF16)
            w_out = w_out_ab[li].astype(BF16)

            q, lat, mk, mv, fq, fkv, fkvb, lf = _proj_ab(xp, pos_p, w, tm_p)
            b3 = lambda a: a.reshape(nb, seq, a.shape[-1])
            cumt = _cumsum_seq(jnp.swapaxes(b3(lf), 1, 2))
            cum = jnp.swapaxes(cumt, 1, 2)
            om = _flash('mla', b3(q), b3(mk), b3(mv), t_attn)
            of = _flash('fox', b3(fq), b3(fkvb), b3(fkvb), t_attn, extra=(cum, cumt))
            xp = _post(xp, [om.reshape(nb * seq, 512)], of.reshape(nb * seq, 512), w_out,
                       g_ffn[layer].reshape(1, d), w_ffn_up[layer].astype(BF16), w_ffn_down[layer].astype(BF16), tm_p)
            mla_p.append(lat.reshape(nb, seq, 160))
            fkv_p.append(fkv.reshape(nb, seq, 2, N_KV, HEAD_DIM))
            flf_p.append(lf.reshape(nb, seq, N_HEADS))

            q, lat, _, _, fq, fkv, _, lf = _proj_ab(xs, pos_s, w, tm_s)
            om = _s_mla(_mla_rows(q, nseq, n_tok), jnp.swapaxes(cache_mla, 2, 3), li, page_table,
                        _pad_new(lat, nseq, n_tok), w, p_pages, n_tok)
            of = _s_fox(_sample_rows(fq, nseq, n_tok), _feat_major(cache_fox_kv), jnp.swapaxes(cache_fox_logf, 2, 3),
                        li, page_table, _pad_new(fkv, nseq, n_tok), _pad_new(lf, nseq, n_tok), p_pages, n_tok)
            xs = _post(xs, [_rows_out(om, nseq, n_tok)], _rows_out(of, nseq, n_tok), w_out,
                       g_ffn[layer].reshape(1, d), w_ffn_up[layer].astype(BF16), w_ffn_down[layer].astype(BF16), tm_s)
            mla_s.append(lat.reshape(nseq, n_tok, 160))
            fkv_s.append(fkv.reshape(nseq, n_tok, 2, N_KV, HEAD_DIM))
            flf_s.append(lf.reshape(nseq, n_tok, N_HEADS))
        else:
            pair = lambda g: jnp.tile(g.reshape(1, -1), (1, 2))
            w = dict(g_mix=gm, w_in=_take_cols(w_in_cd[li], _cd_col_index()).astype(BF16),
                     g_nsa_q=pair(g_nsa_q[li]), g_nsa_ks=pair(g_nsa_ks[li]), g_nsa_kw=pair(g_nsa_kw[li]),
                     b_gate=_pad_lanes(b_nsa_gate[li], 0), g_moba_q=pair(g_moba_q[li]), g_moba_k=pair(g_moba_k[li]),
                     g_nsa_kc=pair(g_nsa_kc[li]))
            w['wka'], w['wkb'], w['w2k'] = _compress_weights(w_nsa_ck1[li], w_nsa_ck2[li])
            w['wva'], w['wvb'], w['w2v'] = _compress_weights(w_nsa_cv1[li], w_nsa_cv2[li])
            w['peka'], w['pekb'] = _compress_pe(nsa_pe_k[li])
            w['peva'], w['pevb'] = _compress_pe(nsa_pe_v[li])
            w_out = w_out_cd[li].astype(BF16)

            (nq, cmpkv, selkv, selb16, winkv, winb16, gates, mq, mqf, mkv, mkvb, kmean) = _proj_cd(xp, w, tm_p, True)
            b3 = lambda a: a.reshape(nb, seq, a.shape[-1])
            kc, vc = _compress_prompt(b3(cmpkv), w)
            o_cmp, sb = _cmp_select(b3(nq), kc, vc, ov_p, b3(gates), t_sel, slopes_n)
            o_sel = _flash('sel', b3(nq), b3(selb16), b3(selb16), t_attn, extra=(e_sel_p, sb), gates=b3(gates),
                           gate_idx=1, slopes=slopes_n)
            o_win = _flash('win', b3(nq), b3(winb16), b3(winb16), t_attn, gates=b3(gates), gate_idx=2,
                           slopes=slopes_n)
            nblk = seq // MOBA_BLOCK
            km = jnp.pad(kmean.reshape(nb, nblk, LANE), ((0, 0), (0, LANE - nblk), (0, 0)))
            sbm = _moba_select(b3(mqf), km, t_sel)
            o_mb = _flash('moba', b3(mq), b3(mkvb), b3(mkvb), t_attn, extra=(e_mb_p, sbm), slopes=slopes_m)
            f2 = lambda a: a.reshape(nb * seq, 512)
            xp = _post(xp, [f2(o_cmp), f2(o_sel), f2(o_win)], f2(o_mb), w_out,
                       g_ffn[layer].reshape(1, d), w_ffn_up[layer].astype(BF16), w_ffn_down[layer].astype(BF16), tm_p)
            kv5 = lambda a, n1, n2: a.reshape(n1, n2, 2, N_KV, HEAD_DIM)
            cmp_p.append(kv5(cmpkv, nb, seq))
            sel_p.append(kv5(selkv, nb, seq))
            win_p.append(kv5(winkv, nb, seq)[:, -min(WINDOW, seq):])
            mb_p.append(kv5(mkv, nb, seq))

            (nq, cmpkv, selkv, _, winkv, _, gates, mq, mqf, mkv, _) = _proj_cd(xs, w, tm_s, False)
            g_rows = gates.reshape(nseq, rows, 3)
            q_n = _sample_rows(nq, nseq, n_tok)
            kc, vc = _compress_sample(_feat_major(cache_nsa_cmp_kv), li, page_table, w, p_pages)
            o_cmp, sb = _s_cmp(q_n, kc, vc, ov_s, grp, grpt, slope_rows_n, g_rows, n_tok, past_len)
            n_ch = n_pages // p_pages
            bpc = p_pages * PAGE // SEL_BLOCK
            sbc = sb[:, :, :n_ch * bpc].reshape(nseq, rows, n_ch, bpc).transpose(0, 2, 1, 3)
            sbc = jnp.pad(sbc, ((0, 0), (0, 0), (0, 0), (0, LANE - bpc)))
            sbn = jnp.pad(sb[:, :, n_ch * bpc:n_ch * bpc + 1], ((0, 0), (0, 0), (0, LANE - 1)))
            o_sel = _s_block(q_n, sbc, sbn, e_sel_s, slope_rows_n, _feat_major(cache_nsa_sel_kv), li, page_table,
                             _pad_new(selkv, nseq, n_tok), p_pages, n_tok, gates=g_rows, gate_idx=1)
            win_all = jnp.concatenate([state_nsa_win_kv[li], kv5(winkv, nseq, n_tok)], axis=1)
            wb = state_nsa_win_kv.shape[2]
            o_win = _s_win(q_n, _feat_major(state_nsa_win_kv), li, _pad_new(winkv, nseq, n_tok), slope_rows_n,
                           g_rows, n_tok, past_len)
            moba_t = _feat_major(cache_moba_kv)
            sbm = _s_moba_score(_sample_rows(mqf, nseq, n_tok), moba_t, li, page_table, p_pages)
            bpc_m = p_pages * PAGE // MOBA_BLOCK
            sbmc = sbm[:, :, :n_ch * bpc_m].reshape(nseq, rows, n_ch, bpc_m).transpose(0, 2, 1, 3)
            sbmc = jnp.pad(sbmc, ((0, 0), (0, 0), (0, 0), (0, LANE - bpc_m)))
            sbmn = jnp.pad(sbm[:, :, n_ch * bpc_m:n_ch * bpc_m + 1], ((0, 0), (0, 0), (0, LANE - 1)))
            o_mb = _s_block(_sample_rows(mq, nseq, n_tok), sbmc, sbmn, e_mb_s, slope_rows_m, moba_t, li,
                            page_table, _pad_new(mkv, nseq, n_tok), p_pages, n_tok)
            r2 = lambda o: _rows_out(o, nseq, n_tok)
            xs = _post(xs, [r2(o_cmp), r2(o_sel), r2(o_win)], r2(o_mb), w_out,
                       g_ffn[layer].reshape(1, d), w_ffn_up[layer].astype(BF16), w_ffn_down[layer].astype(BF16), tm_s)
            cmp_s.append(kv5(cmpkv, nseq, n_tok))
            sel_s.append(kv5(selkv, nseq, n_tok))
            win_s.append(win_all[:, -wb:])
            mb_s.append(kv5(mkv, nseq, n_tok))

    return (xp.reshape(nb, seq, d), xs.reshape(nseq, n_tok, d),
            jnp.stack(mla_p), jnp.stack(mla_s), jnp.stack(fkv_p), jnp.stack(fkv_s), jnp.stack(flf_p),
            jnp.stack(flf_s), jnp.stack(cmp_p), jnp.stack(cmp_s), jnp.stack(sel_p), jnp.stack(sel_s),
            jnp.stack(win_p), jnp.stack(win_s), jnp.stack(mb_p), jnp.stack(mb_s))


def _mla_rows(q, nseq, n_tok):
    q = q.reshape(nseq, n_tok, 1, N_HEADS * LANE)
    return jnp.broadcast_to(q, (nseq, n_tok, N_HEADS, N_HEADS * LANE)).reshape(nseq, n_tok * N_HEADS, N_HEADS * LANE)
```
